```python
import math
import jax, jax.numpy as jnp
from jax import lax
import numpy as np

D_MODEL = 2048
BATCH = 4
SEQ = 4096
DEPTH = 4

N_MIXERS = 2
N_ATTN_LAYERS = (DEPTH + 1) // 2
N_RWKV_LAYERS = DEPTH // 2
N_VRES_LAYERS = max(N_RWKV_LAYERS - 1, 0)

DA_HEAD_DIM = 128
DA_VALUE_DIM = 2 * DA_HEAD_DIM
DA_HEADS = D_MODEL // DA_VALUE_DIM
DA_WIDTH = DA_HEADS * DA_VALUE_DIM
DA_IN_COLS = 4 * DA_WIDTH
ROPE_THETA = 10000.0
Q_BLOCK = 128

RW_HEAD_DIM = 64
RW_HEADS = D_MODEL // RW_HEAD_DIM
RW_WIDTH = RW_HEADS * RW_HEAD_DIM
DECAY_RANK = max(32, int(round(1.8 * D_MODEL ** 0.5 / 32)) * 32)
ICLR_RANK = max(32, int(round(1.8 * D_MODEL ** 0.5 / 32)) * 32)
VRES_RANK = max(32, int(round(1.3 * D_MODEL ** 0.5 / 32)) * 32)
RW_IN_COLS = 4 * RW_WIDTH + DECAY_RANK + ICLR_RANK
RW_SPLITS = [RW_WIDTH, 2 * RW_WIDTH, 3 * RW_WIDTH, 4 * RW_WIDTH, 4 * RW_WIDTH + DECAY_RANK]
GN_EPS = 64e-5
NORM_EPS = 1e-6

kernel_name = "hybrid_diffattn_rwkv7_interleaved"

F32 = jnp.float32


def rms_norm(x, gain, eps=NORM_EPS):
    xf = x.astype(F32)
    y = xf * lax.rsqrt(jnp.mean(xf * xf, axis=-1, keepdims=True) + eps)
    return (y * gain.astype(F32)).astype(x.dtype)


def rope_tables(positions):
    inv = ROPE_THETA ** (-jnp.arange(0, DA_HEAD_DIM, 2, dtype=F32) / DA_HEAD_DIM)
    ang = positions.astype(F32)[..., None] * inv
    ang = jnp.concatenate([ang, ang], axis=-1)
    return jnp.cos(ang)[:, :, None, None, :], jnp.sin(ang)[:, :, None, None, :]


def apply_rope(x, cos, sin):
    xf = x.astype(F32)
    x1, x2 = jnp.split(xf, 2, axis=-1)
    rot = jnp.concatenate([-x2, x1], axis=-1)
    return (xf * cos + rot * sin).astype(x.dtype)


def causal_diff_attention(q, k, v, lam):
    B, T, H, _, dh = q.shape
    dv = v.shape[-1]
    nb = T // Q_BLOCK
    scale = dh ** -0.5
    qb = jnp.moveaxis(q.reshape(B, nb, Q_BLOCK, H, 2, dh), 1, 0)
    key_pos = jnp.arange(T)
    vf = v.astype(F32)
    neg = jnp.finfo(F32).min

    def one_block(args):
        j, qj = args
        s = jnp.einsum('bqhcd,bkhcd->bhcqk', qj, k).astype(F32) * scale
        q_pos = j * Q_BLOCK + jnp.arange(Q_BLOCK)
        mask = key_pos[None, :] <= q_pos[:, None]
        p = jax.nn.softmax(jnp.where(mask, s, neg), axis=-1)
        a = p[:, :, 0] - lam * p[:, :, 1]
        return jnp.einsum('bhqk,bkhd->bqhd', a, vf)

    out = lax.map(one_block, (jnp.arange(nb), qb))
    return jnp.moveaxis(out, 0, 1).reshape(B, T, H, dv)


def diff_attention_mixer(h, cos, sin, w_in, q_gain, k_gain, lam_q1, lam_k1, lam_q2, lam_k2,
                         subln_w, w_out, lambda_init):
    B, T, _ = h.shape
    proj = h @ w_in
    q, k, v, gate = jnp.split(proj, 4, axis=-1)
    q = q.reshape(B, T, DA_HEADS, 2, DA_HEAD_DIM)
    k = k.reshape(B, T, DA_HEADS, 2, DA_HEAD_DIM)
    v = v.reshape(B, T, DA_HEADS, DA_VALUE_DIM)
    q = apply_rope(rms_norm(q, q_gain), cos, sin)
    k = apply_rope(rms_norm(k, k_gain), cos, sin)
    lam = (jnp.exp(jnp.sum(lam_q1.astype(F32) * lam_k1.astype(F32)))
           - jnp.exp(jnp.sum(lam_q2.astype(F32) * lam_k2.astype(F32))) + lambda_init)
    o = causal_diff_attention(q, k, v, lam)
    o = rms_norm(o, subln_w) * (1.0 - lambda_init)
    o = o.reshape(B, T, DA_WIDTH).astype(h.dtype) * jax.nn.silu(gate)
    return o @ w_out


def token_shift(p):
    return jnp.pad(p, ((0, 0), (1, 0), (0, 0)))[:, :-1]


def wkv7_scan(r, w, k, v, a, b):
    B, T, H, N = r.shape
    xs = tuple(jnp.moveaxis(t, 1, 0) for t in (r, w, k, v, a, b))

    def step(S, inp):
        r_t, w_t, k_t, v_t, a_t, b_t = inp
        sa = jnp.einsum('bhvk,bhk->bhv', S, a_t)
        S = S * w_t[:, :, None, :] + sa[..., None] * b_t[:, :, None, :] + v_t[..., None] * k_t[:, :, None, :]
        y = jnp.einsum('bhvk,bhk->bhv', S, r_t)
        return S, y

    _, ys = lax.scan(step, jnp.zeros((B, H, N, N), F32), xs)
    return jnp.moveaxis(ys, 0, 1)


def rwkv7_mixer(h, v_first, w_in, mu, w0, decay_up, a0, iclr_up, k_k, k_a, r_k, gn_w, gn_b,
                w_out, vres):
    B, T, _ = h.shape
    p = h @ w_in
    p = p + (token_shift(p) - p) * mu
    r, k, v, g, dw, da = jnp.split(p, RW_SPLITS, axis=-1)
    r, k, v = r.astype(F32), k.astype(F32), v.astype(F32)
    w_log = -jax.nn.softplus(-(w0.astype(F32) + jnp.tanh(dw.astype(F32)) @ decay_up.astype(F32))) - 0.5
    decay = jnp.exp(-jnp.exp(w_log))
    a = jax.nn.sigmoid(a0.astype(F32) + da.astype(F32) @ iclr_up.astype(F32))
    if vres is None:
        v_first = v
    else:
        vd_w, vd_mu, v0, vu_w = vres
        pv = h @ vd_w
        pv = pv + (token_shift(pv) - pv) * vd_mu
        v = v + (v_first - v) * jax.nn.sigmoid(v0.astype(F32) + pv.astype(F32) @ vu_w.astype(F32))
    heads = lambda t: t.reshape(B, T, RW_HEADS, RW_HEAD_DIM)
    kk = heads(k * k_k.astype(F32))
    kk = kk / jnp.maximum(jnp.sqrt(jnp.sum(kk * kk, axis=-1, keepdims=True)), 1e-12)
    k = k * (1.0 + (a - 1.0) * k_a.astype(F32))
    rh, kh, vh, ah = heads(r), heads(k), heads(v), heads(a)
    y = wkv7_scan(rh, heads(decay), kh, vh, -kk, kk * ah)
    mean = jnp.mean(y, axis=-1, keepdims=True)
    var = jnp.mean(jnp.square(y - mean), axis=-1, keepdims=True)
    y = (y - mean) * lax.rsqrt(var + GN_EPS)
    y = y * gn_w.astype(F32).reshape(RW_HEADS, RW_HEAD_DIM) + gn_b.astype(F32).reshape(RW_HEADS, RW_HEAD_DIM)
    y = y + jnp.sum(rh * kh * r_k.astype(F32), axis=-1, keepdims=True) * vh
    y = y.reshape(B, T, RW_WIDTH).astype(h.dtype) * jax.nn.silu(g)
    return y @ w_out, v_first


def setup_inputs(seed: int = 0) -> dict:
    key = jax.random.key(seed)
    ks = jax.random.split(key, 32)
    nrm = lambda k, s, sc: jax.random.normal(k, s, F32) * sc
    NA, NR, NV = N_ATTN_LAYERS, N_RWKV_LAYERS, N_VRES_LAYERS
    x = jax.random.normal(ks[0], (BATCH, SEQ, D_MODEL), F32)
    offs = jax.random.randint(ks[1], (BATCH, 1), 0, 1024, dtype=jnp.int32)
    positions = (offs + jnp.arange(SEQ, dtype=jnp.int32)[None, :]).astype(jnp.int32)
    return {
        "x": x,
        "positions": positions,
        "norm_w": 1.0 + nrm(ks[2], (DEPTH, D_MODEL), 0.02),
        "da_w_in": nrm(ks[3], (NA, D_MODEL, DA_IN_COLS), D_MODEL ** -0.5),
        "da_q_gain": 1.0 + nrm(ks[4], (NA, DA_HEAD_DIM), 0.02),
        "da_k_gain": 1.0 + nrm(ks[5], (NA, DA_HEAD_DIM), 0.02),
        "da_lam_q1": nrm(ks[6], (NA, DA_HEAD_DIM), 0.1),
        "da_lam_k1": nrm(ks[7], (NA, DA_HEAD_DIM), 0.1),
        "da_lam_q2": nrm(ks[8], (NA, DA_HEAD_DIM), 0.1),
        "da_lam_k2": nrm(ks[9], (NA, DA_HEAD_DIM), 0.1),
        "da_subln_w": 1.0 + nrm(ks[10], (NA, DA_VALUE_DIM), 0.02),
        "da_w_out": nrm(ks[11], (NA, DA_WIDTH, D_MODEL), DA_WIDTH ** -0.5),
        "rw_w_in": nrm(ks[12], (NR, D_MODEL, RW_IN_COLS), D_MODEL ** -0.5),
        "rw_mu": jax.random.uniform(ks[13], (NR, RW_IN_COLS), F32),
        "rw_w0": jax.random.uniform(ks[14], (NR, RW_WIDTH), F32, -6.0, 1.0),
        "rw_decay_up": nrm(ks[15], (NR, DECAY_RANK, RW_WIDTH), 0.5 * DECAY_RANK ** -0.5),
        "rw_a0": nrm(ks[16], (NR, RW_WIDTH), 0.5),
        "rw_iclr_up": nrm(ks[17], (NR, ICLR_RANK, RW_WIDTH), 0.5 * ICLR_RANK ** -0.5),
        "rw_k_k": 0.85 + nrm(ks[18], (NR, RW_WIDTH), 0.05),
        "rw_k_a": 1.0 + nrm(ks[19], (NR, RW_WIDTH), 0.05),
        "rw_r_k": nrm(ks[20], (NR, RW_HEADS, RW_HEAD_DIM), 0.1),
        "rw_gn_w": 1.0 + nrm(ks[21], (NR, RW_WIDTH), 0.02),
        "rw_gn_b": nrm(ks[22], (NR, RW_WIDTH), 0.01),
        "rw_w_out": nrm(ks[23], (NR, RW_WIDTH, D_MODEL), RW_WIDTH ** -0.5),
        "rw_vres_down": nrm(ks[24], (NV, D_MODEL, VRES_RANK), D_MODEL ** -0.5),
        "rw_vres_mu": jax.random.uniform(ks[25], (NV, VRES_RANK), F32),
        "rw_v0": 1.0 + nrm(ks[26], (NV, RW_WIDTH), 0.2),
        "rw_vres_up": nrm(ks[27], (NV, VRES_RANK, RW_WIDTH), VRES_RANK ** -0.5),
    }


def reference(x, positions, norm_w, da_w_in, da_q_gain, da_k_gain, da_lam_q1, da_lam_k1,
              da_lam_q2, da_lam_k2, da_subln_w, da_w_out, rw_w_in, rw_mu, rw_w0, rw_decay_up,
              rw_a0, rw_iclr_up, rw_k_k, rw_k_a, rw_r_k, rw_gn_w, rw_gn_b, rw_w_out,
              rw_vres_down, rw_vres_mu, rw_v0, rw_vres_up):
    cos, sin = rope_tables(positions)
    v_first = None
    for i in range(DEPTH):
        hn = rms_norm(x, norm_w[i])
        j = i // N_MIXERS
        if i % N_MIXERS == 0:
            lambda_init = 0.8 - 0.6 * math.exp(-0.3 * i)
            out = diff_attention_mixer(hn, cos, sin, da_w_in[j], da_q_gain[j], da_k_gain[j],
                                       da_lam_q1[j], da_lam_k1[j], da_lam_q2[j], da_lam_k2[j],
                                       da_subln_w[j], da_w_out[j], lambda_init)
        else:
            vres = None if j == 0 else (rw_vres_down[j - 1], rw_vres_mu[j - 1], rw_v0[j - 1], rw_vres_up[j - 1])
            out, v_first = rwkv7_mixer(hn, v_first, rw_w_in[j], rw_mu[j], rw_w0[j], rw_decay_up[j],
                                       rw_a0[j], rw_iclr_up[j], rw_k_k[j], rw_k_a[j], rw_r_k[j],
                                       rw_gn_w[j], rw_gn_b[j], rw_w_out[j], vres)
        x = x + out.astype(x.dtype)
    return x
```

```python
import functools
import math

import jax
import jax.numpy as jnp
from jax import lax
from jax.experimental import pallas as pl
from jax.experimental.pallas import tpu as pltpu

F32 = jnp.float32
BF16 = jnp.bfloat16

LANES = 128
DA_HEAD_DIM = 128
DA_VALUE_DIM = 2 * DA_HEAD_DIM
RW_HEAD_DIM = 64
RW_PAIR = 2 * RW_HEAD_DIM
WKV_CHUNK = 64
LORA_PAD = 128
ROPE_THETA = 10000.0
NORM_EPS = 1e-6
GN_EPS = 64e-5
NEG_BIG = -1e30
VMEM_LIMIT = 56 * 1024 * 1024


def _params(sem):
    return pltpu.CompilerParams(dimension_semantics=sem, vmem_limit_bytes=VMEM_LIMIT)


def _dot(a, b):
    return jnp.dot(a, b, preferred_element_type=F32)


def _dot_nt(a, b):
    return lax.dot_general(a, b, (((1,), (1,)), ((), ())), preferred_element_type=F32)


def _split2(x):
    hi = x.astype(BF16)
    lo = (x - hi.astype(F32)).astype(BF16)
    return hi, lo


def _split3(x):
    hi = x.astype(BF16)
    r1 = x - hi.astype(F32)
    mid = r1.astype(BF16)
    lo = (r1 - mid.astype(F32)).astype(BF16)
    return hi, mid, lo


def _dot3(a, b, nt=False):
    d = _dot_nt if nt else _dot
    ah, al = _split2(a)
    bh, bl = _split2(b)
    return d(ah, bh) + (d(ah, bl) + d(al, bh))


def _dot1(a, b, nt=False):
    d = _dot_nt if nt else _dot
    return d(a.astype(BF16), b.astype(BF16))


def _segsum(x, ones_blk):
    hi, mid, lo = _split3(x)
    return _dot(hi, ones_blk) + (_dot(mid, ones_blk) + _dot(lo, ones_blk))


def _half_ones():
    r = lax.broadcasted_iota(jnp.int32, (LANES, LANES), 0)
    c = lax.broadcasted_iota(jnp.int32, (LANES, LANES), 1)
    return jnp.where((r >= RW_HEAD_DIM) == (c >= RW_HEAD_DIM), 1.0, 0.0).astype(BF16)


def _sigmoid(x):
    return 1.0 / (1.0 + jnp.exp(-x))


def _norm_mm_kernel(x_ref, g_ref, w_ref, mu_ref, o_ref, xn_ref, prev_ref, *, shift, rows_per_seq):
    i = pl.program_id(0)
    j = pl.program_id(1)
    tm = x_ref.shape[0]

    @pl.when(j == 0)
    def _():
        x = x_ref[...]
        ms = jnp.mean(x * x, axis=-1, keepdims=True)
        xn_ref[...] = (x * lax.rsqrt(ms + NORM_EPS) * g_ref[...]).astype(BF16)

    acc = _dot(xn_ref[...], w_ref[...])
    if shift:
        @pl.when(i == 0)
        def _():
            prev_ref[j] = jnp.zeros(prev_ref.shape[1:], F32)

        seq_start = (i * tm) % rows_per_seq == 0
        prev = jnp.where(seq_start, 0.0, prev_ref[j])
        row = lax.broadcasted_iota(jnp.int32, acc.shape, 0)
        shifted = jnp.where(row == 0, prev, pltpu.roll(acc, 1, axis=0))
        prev_ref[j] = acc[tm - 1:tm, :]
        acc = acc + (shifted - acc) * mu_ref[...]
    o_ref[...] = acc


def _norm_matmul(x, gain, w_bf16, mu, *, shift, rows_per_seq, tm, tn):
    m, d = x.shape
    n = w_bf16.shape[1]
    assert m % tm == 0 and n % tn == 0 and rows_per_seq % tm == 0
    kern = functools.partial(_norm_mm_kernel, shift=shift, rows_per_seq=rows_per_seq)
    return pl.pallas_call(
        kern,
        grid=(m // tm, n // tn),
        in_specs=[
            pl.BlockSpec((tm, d), lambda i, j: (i, 0)),
            pl.BlockSpec((1, d), lambda i, j: (0, 0)),
            pl.BlockSpec((d, tn), lambda i, j: (0, j)),
            pl.BlockSpec((1, tn), lambda i, j: (0, j)),
        ],
        out_specs=pl.BlockSpec((tm, tn), lambda i, j: (i, j)),
        out_shape=jax.ShapeDtypeStruct((m, n), F32),
        scratch_shapes=[pltpu.VMEM((tm, d), BF16), pltpu.VMEM((n // tn, 1, tn), F32)],
        compiler_params=_params(("arbitrary", "arbitrary")),
        name="norm_matmul_shift" if shift else "norm_matmul",
    )(x, gain.reshape(1, d), w_bf16, mu.reshape(1, n))


def _mm_res_kernel(a_ref, w_ref, x_ref, o_ref):
    o_ref[...] = x_ref[...] + _dot(a_ref[...], w_ref[...])


def _matmul_residual(a_bf16, w_bf16, x, *, tm, tn):
    m, k = a_bf16.shape
    n = w_bf16.shape[1]
    assert m % tm == 0 and n % tn == 0
    return pl.pallas_call(
        _mm_res_kernel,
        grid=(m // tm, n // tn),
        in_specs=[
            pl.BlockSpec((tm, k), lambda i, j: (i, 0)),
            pl.BlockSpec((k, tn), lambda i, j: (0, j)),
            pl.BlockSpec((tm, tn), lambda i, j: (i, j)),
        ],
        out_specs=pl.BlockSpec((tm, tn), lambda i, j: (i, j)),
        out_shape=jax.ShapeDtypeStruct((m, n), F32),
        compiler_params=_params(("parallel", "arbitrary")),
        name="out_proj_residual",
    )(a_bf16, w_bf16, x)


def _rope_kernel(pos_ref, inv_ref, cos_ref, sin_ref):
    ang = pos_ref[...].astype(F32) * inv_ref[...]
    cos_ref[...] = jnp.cos(ang)
    sin_ref[...] = jnp.sin(ang)


def _rope_tables(positions_flat, *, tr):
    m = positions_flat.shape[0]
    inv = ROPE_THETA ** (-jnp.arange(0, DA_HEAD_DIM, 2, dtype=F32) / DA_HEAD_DIM)
    inv = jnp.concatenate([inv, inv]).reshape(1, DA_HEAD_DIM)
    return pl.pallas_call(
        _rope_kernel,
        grid=(m // tr,),
        in_specs=[pl.BlockSpec((tr, 1), lambda i: (i, 0)), pl.BlockSpec((1, DA_HEAD_DIM), lambda i: (0, 0))],
        out_specs=[pl.BlockSpec((tr, DA_HEAD_DIM), lambda i: (i, 0))] * 2,
        out_shape=[jax.ShapeDtypeStruct((m, DA_HEAD_DIM), F32)] * 2,
        compiler_params=_params(("parallel",)),
        name="rope_tables",
    )(positions_flat.reshape(m, 1), inv)


def _qkv_prep_kernel(q_ref, k_ref, v_ref, cos_ref, sin_ref, qg_ref, kg_ref, qo_ref, ko_ref, vo_ref, *, scale):
    cos = cos_ref[...]
    lane = lax.broadcasted_iota(jnp.int32, cos.shape, 1)
    sin_signed = jnp.where(lane < DA_HEAD_DIM // 2, -sin_ref[...], sin_ref[...])
    width = q_ref.shape[1]

    def prep(src, gain, dst, mult):
        for c in range(width // DA_HEAD_DIM):
            sl = slice(c * DA_HEAD_DIM, (c + 1) * DA_HEAD_DIM)
            x = src[:, sl]
            y = x * lax.rsqrt(jnp.mean(x * x, axis=-1, keepdims=True) + NORM_EPS) * gain
            rot = pltpu.roll(y, DA_HEAD_DIM // 2, axis=1)
            dst[:, sl] = ((y * cos + rot * sin_signed) * mult).astype(BF16)

    prep(q_ref, qg_ref[...], qo_ref, scale)
    prep(k_ref, kg_ref[...], ko_ref, 1.0)
    vo_ref[...] = v_ref[...].astype(BF16)


def _qkv_prep(proj, cos, sin, q_gain, k_gain, *, width, tr):
    m = proj.shape[0]
    blk = lambda c: pl.BlockSpec((tr, width), lambda i: (i, c))
    rope = pl.BlockSpec((tr, DA_HEAD_DIM), lambda i: (i, 0))
    gain = pl.BlockSpec((1, DA_HEAD_DIM), lambda i: (0, 0))
    kern = functools.partial(_qkv_prep_kernel, scale=DA_HEAD_DIM ** -0.5)
    return pl.pallas_call(
        kern,
        grid=(m // tr,),
        in_specs=[blk(0), blk(1), blk(2), rope, rope, gain, gain],
        out_specs=[pl.BlockSpec((tr, width), lambda i: (i, 0))] * 3,
        out_shape=[jax.ShapeDtypeStruct((m, width), BF16)] * 3,
        compiler_params=_params(("parallel",)),
        name="qkv_prep",
    )(proj, proj, proj, cos, sin, q_gain.reshape(1, -1), k_gain.reshape(1, -1))


def _flash_kernel(q_ref, k_ref, v_ref, gate_ref, lq1_ref, lk1_ref, lq2_ref, lk2_ref, sub_ref, o_ref,
                  m_ref, l_ref, acc_ref, *, lambda_init):
    qi = pl.program_id(2)
    ki = pl.program_id(3)
    tq = q_ref.shape[0]
    tk = k_ref.shape[0]

    @pl.when(ki == 0)
    def _():
        m_ref[...] = jnp.full(m_ref.shape, NEG_BIG, F32)
        l_ref[...] = jnp.zeros(l_ref.shape, F32)
        acc_ref[...] = jnp.zeros(acc_ref.shape, F32)

    def step(masked):
        v = v_ref[...]
        if masked:
            row = lax.broadcasted_iota(jnp.int32, (tq, tk), 0)
            col = lax.broadcasted_iota(jnp.int32, (tq, tk), 1)
            keep = col <= row
        for c in range(2):
            sl = slice(c * DA_HEAD_DIM, (c + 1) * DA_HEAD_DIM)
            s = _dot_nt(q_ref[:, sl], k_ref[:, sl])
            if masked:
                s = jnp.where(keep, s, NEG_BIG)
            m_prev = m_ref[c]
            m_new = jnp.maximum(m_prev, jnp.max(s, axis=-1, keepdims=True))
            alpha = jnp.exp(m_prev - m_new)
            p = jnp.exp(s - m_new)
            l_ref[c] = alpha * l_ref[c] + jnp.sum(p, axis=-1, keepdims=True)
            acc_ref[c] = alpha * acc_ref[c] + _dot(p.astype(BF16), v)
            m_ref[c] = m_new

    @pl.when(ki < qi)
    def _():
        step(False)

    @pl.when(ki == qi)
    def _():
        step(True)
        lam = (jnp.exp(jnp.sum(lq1_ref[...] * lk1_ref[...], axis=-1, keepdims=True))
               - jnp.exp(jnp.sum(lq2_ref[...] * lk2_ref[...], axis=-1, keepdims=True)) + lambda_init)
        o = acc_ref[0] / l_ref[0] - lam * (acc_ref[1] / l_ref[1])
        o = o * lax.rsqrt(jnp.mean(o * o, axis=-1, keepdims=True) + NORM_EPS) * sub_ref[...]
        g = gate_ref[...]
        o_ref[...] = (o * (1.0 - lambda_init) * (g * _sigmoid(g))).astype(BF16)


def _flash_diff_attention(q, k, v, proj, lam_q1, lam_k1, lam_q2, lam_k2, subln_w, *, batch, seq, heads,
                          lambda_init, tq):
    m = q.shape[0]
    nq = seq // tq
    gate_col0 = 3 * heads
    row_q = lambda b, h, qi, ki: (b * nq + qi, h)
    row_k = lambda b, h, qi, ki: (b * nq + jnp.minimum(ki, qi), h)
    vec = lambda n: pl.BlockSpec((1, n), lambda b, h, qi, ki: (0, 0))
    kern = functools.partial(_flash_kernel, lambda_init=lambda_init)
    return pl.pallas_call(
        kern,
        grid=(batch, heads, nq, nq),
        in_specs=[
            pl.BlockSpec((tq, DA_VALUE_DIM), row_q),
            pl.BlockSpec((tq, DA_VALUE_DIM), row_k),
            pl.BlockSpec((tq, DA_VALUE_DIM), row_k),
            pl.BlockSpec((tq, DA_VALUE_DIM), lambda b, h, qi, ki: (b * nq + qi, gate_col0 + h)),
            vec(DA_HEAD_DIM), vec(DA_HEAD_DIM), vec(DA_HEAD_DIM), vec(DA_HEAD_DIM), vec(DA_VALUE_DIM),
        ],
        out_specs=pl.BlockSpec((tq, DA_VALUE_DIM), row_q),
        out_shape=jax.ShapeDtypeStruct((m, heads * DA_VALUE_DIM), BF16),
        scratch_shapes=[pltpu.VMEM((2, tq, 1), F32), pltpu.VMEM((2, tq, 1), F32),
                        pltpu.VMEM((2, tq, DA_VALUE_DIM), F32)],
        compiler_params=_params(("parallel", "parallel", "parallel", "arbitrary")),
        name="flash_diff_attention",
    )(q, k, v, proj, lam_q1.reshape(1, -1), lam_k1.reshape(1, -1), lam_q2.reshape(1, -1),
      lam_k2.reshape(1, -1), subln_w.reshape(1, -1))


def _rwkv_prep_kernel(r_ref, k_ref, v_ref, lora_ref, vfirst_ref, w0_ref, a0_ref, kk_ref, ka_ref, rk_ref, v0_ref,
                      dup_ref, iup_ref, vup_ref,
                      lw_ref, ko_ref, vo_ref, an_ref, bb_ref, bonus_ref, *, vres):
    ones_blk = _half_ones()
    lora = lora_ref[...]
    dwt = jnp.tanh(lora[:, 0:LORA_PAD]).astype(BF16)
    da = lora[:, LORA_PAD:2 * LORA_PAD].astype(BF16)
    z = -(w0_ref[...] + _dot(dwt, dup_ref[...]))
    w_log = -(jnp.maximum(z, 0.0) + jnp.log1p(jnp.exp(-jnp.abs(z)))) - 0.5
    lw_ref[...] = -jnp.exp(w_log)
    a = _sigmoid(a0_ref[...] + _dot(da, iup_ref[...]))
    r = r_ref[...]
    k = k_ref[...]
    v = v_ref[...]
    if vres:
        pv = lora[:, 2 * LORA_PAD:3 * LORA_PAD].astype(BF16)
        v = v + (vfirst_ref[...] - v) * _sigmoid(v0_ref[...] + _dot(pv, vup_ref[...]))
    vo_ref[...] = v
    kk = k * kk_ref[...]
    k_mod = k * (1.0 + (a - 1.0) * ka_ref[...])
    ko_ref[...] = k_mod
    rkr = r * k_mod * rk_ref[...]
    for c in range(r.shape[1] // LANES):
        sl = slice(c * LANES, (c + 1) * LANES)
        kc = kk[:, sl]
        nrm = jnp.maximum(jnp.sqrt(_segsum(kc * kc, ones_blk)), 1e-12)
        kn = kc / nrm
        an_ref[:, sl] = -kn
        bb_ref[:, sl] = kn * a[:, sl]
        bonus_ref[:, sl] = _segsum(rkr[:, sl], ones_blk) * v[:, sl]


def _rwkv_prep(pm, lora, v_first, w0, a0, k_k, k_a, r_k, v0, dup, iup, vup, *, width, vres, tr, tc):
    m = pm.shape[0]
    ncb = width // tc
    col = lambda c0: pl.BlockSpec((tr, tc), lambda i, j: (i, c0 * ncb + j))
    vec = pl.BlockSpec((1, tc), lambda i, j: (0, j))
    up = pl.BlockSpec((LORA_PAD, tc), lambda i, j: (0, j))
    kern = functools.partial(_rwkv_prep_kernel, vres=vres)
    if v_first is None:
        v_first, vf_spec = pm, col(2)
    else:
        vf_spec = col(0)
    row = lambda a: a.reshape(1, width)
    return pl.pallas_call(
        kern,
        grid=(m // tr, ncb),
        in_specs=[col(0), col(1), col(2), pl.BlockSpec((tr, lora.shape[1]), lambda i, j: (i, 0)), vf_spec,
                  vec, vec, vec, vec, vec, vec, up, up, up],
        out_specs=[pl.BlockSpec((tr, tc), lambda i, j: (i, j))] * 6,
        out_shape=[jax.ShapeDtypeStruct((m, width), F32)] * 6,
        compiler_params=_params(("parallel", "parallel")),
        name="rwkv_prep",
    )(pm, pm, pm, lora, v_first, row(w0), row(a0), row(k_k), row(k_a), row(r_k), row(v0), dup, iup, vup)


def _unit_lower_inverse(low, same16, same32, eye):
    d1 = jnp.where(same16, low, 0.0)
    d2 = _dot3(d1, d1)
    d4 = _dot3(d2, d2)
    d8 = _dot3(d4, d4)
    x = eye + d1 + d2 + _dot3(d1, d2)
    x = x + _dot3(x, d4)
    x = x + _dot3(x, d8)
    n32 = jnp.where(jnp.logical_and(same32, jnp.logical_not(same16)), low, 0.0)
    x = x + _dot3(_dot3(x, n32), x)
    n64 = jnp.where(same32, 0.0, low)
    x = x + _dot3(_dot3(x, n64), x)
    return x


def _wkv_kernel(r_ref, lw_ref, k_ref, v_ref, an_ref, bb_ref, bonus_ref, g_ref, gnw_ref, gnb_ref, o_ref, s_ref):
    c = pl.program_id(2)
    groups = r_ref.shape[1] // LANES
    ch = WKV_CHUNK
    n2 = 2 * ch

    @pl.when(c == 0)
    def _():
        s_ref[...] = jnp.zeros(s_ref.shape, F32)

    row = lax.broadcasted_iota(jnp.int32, (n2, n2), 0)
    col = lax.broadcasted_iota(jnp.int32, (n2, n2), 1)
    strict = col < row
    incl = col <= row
    same16 = (row >> 4) == (col >> 4)
    same32 = (row >> 5) == (col >> 5)
    eye = jnp.where(row == col, 1.0, 0.0)
    r64 = lax.broadcasted_iota(jnp.int32, (ch, ch), 0)
    c64 = lax.broadcasted_iota(jnp.int32, (ch, ch), 1)
    tri = jnp.where(c64 <= r64, 1.0, 0.0)
    lane = lax.broadcasted_iota(jnp.int32, (ch, LANES), 1)
    m0 = jnp.where(lane < RW_HEAD_DIM, 1.0, 0.0)
    m1 = 1.0 - m0
    ones_blk = _half_ones()

    def stack2(x):
        return jnp.concatenate([x * m0, x * m1], axis=0)

    for gi in range(groups):
        sl = slice(gi * LANES, (gi + 1) * LANES)
        lw = lw_ref[:, sl]
        cum = _dot3(tri, lw)
        g_in = jnp.exp(cum)
        g_inv = jnp.exp(-cum)
        g_ex = jnp.exp(cum - lw)
        v = v_ref[:, sl]
        a2 = stack2(an_ref[:, sl] * g_ex)
        b2 = stack2(bb_ref[:, sl] * g_inv)
        k2 = stack2(k_ref[:, sl] * g_inv)
        r2 = stack2(r_ref[:, sl] * g_in)
        v2 = stack2(v)
        g_end = g_in[ch - 1:ch, :]

        sc = _dot3(jnp.concatenate([a2, r2], axis=0), jnp.concatenate([b2, k2], axis=0), nt=True)
        low_ab = jnp.where(strict, sc[:n2, :n2], 0.0)
        low_ak = jnp.where(strict, sc[:n2, n2:], 0.0)
        low_rb = jnp.where(incl, sc[n2:, :n2], 0.0)
        low_rk = jnp.where(incl, sc[n2:, n2:], 0.0)

        tinv = _unit_lower_inverse(low_ab, same16, same32, eye)
        w1 = _dot3(low_ak, v2)
        au = _dot3(tinv, jnp.concatenate([a2, w1], axis=1))
        ry = _dot3(low_rb, au)
        rh = r2 + ry[:, :n2]
        yh = ry[:, n2:] + _dot3(low_rk, v2)

        s = s_ref[gi]
        y2 = _dot3(rh, s, nt=True) + yh
        y = y2[:ch] + y2[ch:]

        p = _dot3(au[:, :n2].T, b2)
        q = _dot3(jnp.concatenate([au[:, n2:], v2], axis=0).T, jnp.concatenate([b2, k2], axis=0))
        s_ref[gi] = (s + _dot3(s, p) + q) * g_end

        inv_n = 1.0 / RW_HEAD_DIM
        mean = _segsum(y, ones_blk) * inv_n
        d = y - mean
        var = _segsum(d * d, ones_blk) * inv_n
        yn = d * lax.rsqrt(var + GN_EPS) * gnw_ref[:, sl] + gnb_ref[:, sl]
        g = g_ref[:, sl]
        o_ref[:, sl] = ((yn + bonus_ref[:, sl]) * (g * _sigmoid(g))).astype(BF16)


def _wkv(pm, lw, k, v, an, bb, bonus, gn_w, gn_b, *, batch, seq, width, groups):
    m = pm.shape[0]
    nc = seq // WKV_CHUNK
    tc = groups * LANES
    ncb = width // tc
    blk = pl.BlockSpec((WKV_CHUNK, tc), lambda b, j, c: (b * nc + c, j))
    pm_blk = lambda c0: pl.BlockSpec((WKV_CHUNK, tc), lambda b, j, c: (b * nc + c, c0 * ncb + j))
    vec = pl.BlockSpec((1, tc), lambda b, j, c: (0, j))
    return pl.pallas_call(
        _wkv_kernel,
        grid=(batch, ncb, nc),
        in_specs=[pm_blk(0), blk, blk, blk, blk, blk, blk, pm_blk(3), vec, vec],
        out_specs=blk,
        out_shape=jax.ShapeDtypeStruct((m, width), BF16),
        scratch_shapes=[pltpu.VMEM((groups, 2 * WKV_CHUNK, 2 * WKV_CHUNK), F32)],
        compiler_params=_params(("parallel", "parallel", "arbitrary")),
        name="wkv7_chunked",
    )(pm, lw, k, v, an, bb, bonus, pm, gn_w.reshape(1, width), gn_b.reshape(1, width))


def _tiles(m, seq):
    tm = min(512, seq)
    return dict(tm=tm, tn=min(1024, 4 * DA_VALUE_DIM), tr=min(256, seq), tq=min(512, seq))


def _attention_layer(x, cos, sin, norm_w, w_in, q_gain, k_gain, lq1, lk1, lq2, lk2, subln_w, w_out, lambda_init,
                     *, batch, seq, t):
    d = x.shape[1]
    heads = d // DA_VALUE_DIM
    proj = _norm_matmul(x, norm_w, w_in.astype(BF16), jnp.zeros((w_in.shape[1],), F32), shift=False,
                        rows_per_seq=seq, tm=t["tm"], tn=t["tn"])
    q, k, v = _qkv_prep(proj, cos, sin, q_gain, k_gain, width=d, tr=t["tr"])
    o = _flash_diff_attention(q, k, v, proj, lq1, lk1, lq2, lk2, subln_w, batch=batch, seq=seq, heads=heads,
                              lambda_init=lambda_init, tq=t["tq"])
    return _matmul_residual(o, w_out.astype(BF16), x, tm=t["tm"], tn=min(t["tn"], d))


def _pad_rows(w, rows):
    return jnp.pad(w, ((0, rows - w.shape[0]), (0, 0)))


def _pad_cols(w, cols):
    return jnp.pad(w, ((0, 0), (0, cols - w.shape[1])))


def _rwkv_layer(x, v_first, norm_w, w_in, mu, w0, decay_up, a0, iclr_up, k_k, k_a, r_k, gn_w, gn_b, w_out, vres,
                *, batch, seq, t):
    d = x.shape[1]
    wide = 4 * d
    dr = decay_up.shape[0]
    ir = iclr_up.shape[0]
    lora_w = [_pad_cols(w_in[:, wide:wide + dr], LORA_PAD), _pad_cols(w_in[:, wide + dr:wide + dr + ir], LORA_PAD)]
    lora_mu = [jnp.pad(mu[wide:wide + dr], (0, LORA_PAD - dr)), jnp.pad(mu[wide + dr:wide + dr + ir], (0, LORA_PAD - ir))]
    if vres is not None:
        vd_w, vd_mu, v0, vu_w = vres
        lora_w.append(_pad_cols(vd_w, LORA_PAD))
        lora_mu.append(jnp.pad(vd_mu, (0, LORA_PAD - vd_mu.shape[0])))
        vup = _pad_rows(vu_w, LORA_PAD).astype(BF16)
    else:
        v0 = jnp.zeros((d,), F32)
        vup = jnp.zeros((LORA_PAD, d), BF16)
    lora_w = jnp.concatenate(lora_w, axis=1).astype(BF16)
    lora_mu = jnp.concatenate(lora_mu)
    pm = _norm_matmul(x, norm_w, w_in[:, :wide].astype(BF16), mu[:wide], shift=True, rows_per_seq=seq,
                      tm=t["tm"], tn=t["tn"])
    lora = _norm_matmul(x, norm_w, lora_w, lora_mu, shift=True, rows_per_seq=seq, tm=t["tm"], tn=lora_w.shape[1])
    lw, k, v, an, bb, bonus = _rwkv_prep(
        pm, lora, v_first, w0, a0, k_k, k_a, r_k.reshape(-1), v0,
        _pad_rows(decay_up, LORA_PAD).astype(BF16), _pad_rows(iclr_up, LORA_PAD).astype(BF16), vup,
        width=d, vres=vres is not None, tr=t["tr"], tc=min(512, d))
    o = _wkv(pm, lw, k, v, an, bb, bonus, gn_w, gn_b, batch=batch, seq=seq, width=d, groups=2)
    return _matmul_residual(o, w_out.astype(BF16), x, tm=t["tm"], tn=min(t["tn"], d)), v


def kernel(x, positions, norm_w, da_w_in, da_q_gain, da_k_gain, da_lam_q1, da_lam_k1, da_lam_q2, da_lam_k2, da_subln_w, da_w_out, rw_w_in, rw_mu, rw_w0, rw_decay_up, rw_a0, rw_iclr_up, rw_k_k, rw_k_a, rw_r_k, rw_gn_w, rw_gn_b, rw_w_out, rw_vres_down, rw_vres_mu, rw_v0, rw_vres_up):
    batch, seq, d = x.shape
    depth = norm_w.shape[0]
    m = batch * seq
    t = _tiles(m, seq)
    xf = x.reshape(m, d)
    cos, sin = _rope_tables(positions.reshape(m), tr=t["tr"])
    v_first = None
    for i in range(depth):
        j = i // 2
        if i % 2 == 0:
            lambda_init = 0.8 - 0.6 * math.exp(-0.3 * i)
            xf = _attention_layer(xf, cos, sin, norm_w[i], da_w_in[j], da_q_gain[j], da_k_gain[j], da_lam_q1[j],
                                  da_lam_k1[j], da_lam_q2[j], da_lam_k2[j], da_subln_w[j], da_w_out[j], lambda_init,
                                  batch=batch, seq=seq, t=t)
        else:
            vres = None if j == 0 else (rw_vres_down[j - 1], rw_vres_mu[j - 1], rw_v0[j - 1], rw_vres_up[j - 1])
            xf, v_now = _rwkv_layer(xf, v_first, norm_w[i], rw_w_in[j], rw_mu[j], rw_w0[j], rw_decay_up[j], rw_a0[j],
                                    rw_iclr_up[j], rw_k_k[j], rw_k_a[j], rw_r_k[j], rw_gn_w[j], rw_gn_b[j],
                                    rw_w_out[j], vres, batch=batch, seq=seq, t=t)
            if v_first is None:
                v_first = v_now
    return xf.reshape(batch, seq, d)
```

```python
import functools
import math

import jax
import jax.numpy as jnp
from jax import lax
from jax.experimental import pallas as pl
from jax.experimental.pallas import tpu as pltpu

F32 = jnp.float32
BF16 = jnp.bfloat16

LANES = 128
DA_HEAD_DIM = 128
DA_VALUE_DIM = 2 * DA_HEAD_DIM
RW_HEAD_DIM = 64
RW_PAIR = 2 * RW_HEAD_DIM
WKV_CHUNK = 64
WKV_UNITS = 8
WKV_PASSES = 1
LORA_PAD = 128
ROPE_THETA = 10000.0
NORM_EPS = 1e-6
GN_EPS = 64e-5
NEG_BIG = -1e30
VMEM_LIMIT = 56 * 1024 * 1024


def _params(sem):
    return pltpu.CompilerParams(dimension_semantics=sem, vmem_limit_bytes=VMEM_LIMIT)


def _dot(a, b):
    return jnp.dot(a, b, preferred_element_type=F32)


def _dot_nt(a, b):
    return lax.dot_general(a, b, (((1,), (1,)), ((), ())), preferred_element_type=F32)


def _split2(x):
    hi = x.astype(BF16)
    lo = (x - hi.astype(F32)).astype(BF16)
    return hi, lo


def _split3(x):
    hi = x.astype(BF16)
    r1 = x - hi.astype(F32)
    mid = r1.astype(BF16)
    lo = (r1 - mid.astype(F32)).astype(BF16)
    return hi, mid, lo


def _dot3(a, b, nt=False):
    d = _dot_nt if nt else _dot
    ah, al = _split2(a)
    bh, bl = _split2(b)
    return d(ah, bh) + (d(ah, bl) + d(al, bh))


def _dot1(a, b, nt=False):
    d = _dot_nt if nt else _dot
    return d(a.astype(BF16), b.astype(BF16))


def _segsum(x, ones_blk):
    hi, mid, lo = _split3(x)
    return _dot(hi, ones_blk) + (_dot(mid, ones_blk) + _dot(lo, ones_blk))


def _half_ones():
    r = lax.broadcasted_iota(jnp.int32, (LANES, LANES), 0)
    c = lax.broadcasted_iota(jnp.int32, (LANES, LANES), 1)
    return jnp.where((r >= RW_HEAD_DIM) == (c >= RW_HEAD_DIM), 1.0, 0.0).astype(BF16)


def _sigmoid(x):
    return 1.0 / (1.0 + jnp.exp(-x))


def _norm_mm_kernel(x_ref, g_ref, w_ref, mu_ref, o_ref, xn_ref, prev_ref, *, shift, rows_per_seq):
    i = pl.program_id(0)
    j = pl.program_id(1)
    tm = x_ref.shape[0]

    @pl.when(j == 0)
    def _():
        x = x_ref[...]
        ms = jnp.mean(x * x, axis=-1, keepdims=True)
        xn_ref[...] = (x * lax.rsqrt(ms + NORM_EPS) * g_ref[...]).astype(BF16)

    acc = _dot(xn_ref[...], w_ref[...])
    if shift:
        @pl.when(i == 0)
        def _():
            prev_ref[j] = jnp.zeros(prev_ref.shape[1:], F32)

        seq_start = (i * tm) % rows_per_seq == 0
        prev = jnp.where(seq_start, 0.0, prev_ref[j])
        row = lax.broadcasted_iota(jnp.int32, acc.shape, 0)
        shifted = jnp.where(row == 0, prev, pltpu.roll(acc, 1, axis=0))
        prev_ref[j] = acc[tm - 1:tm, :]
        acc = acc + (shifted - acc) * mu_ref[...]
    o_ref[...] = acc


def _norm_matmul(x, gain, w_bf16, mu, *, shift, rows_per_seq, tm, tn):
    m, d = x.shape
    n = w_bf16.shape[1]
    assert m % tm == 0 and n % tn == 0 and rows_per_seq % tm == 0
    kern = functools.partial(_norm_mm_kernel, shift=shift, rows_per_seq=rows_per_seq)
    return pl.pallas_call(
        kern,
        grid=(m // tm, n // tn),
        in_specs=[
            pl.BlockSpec((tm, d), lambda i, j: (i, 0)),
            pl.BlockSpec((1, d), lambda i, j: (0, 0)),
            pl.BlockSpec((d, tn), lambda i, j: (0, j)),
            pl.BlockSpec((1, tn), lambda i, j: (0, j)),
        ],
        out_specs=pl.BlockSpec((tm, tn), lambda i, j: (i, j)),
        out_shape=jax.ShapeDtypeStruct((m, n), F32),
        scratch_shapes=[pltpu.VMEM((tm, d), BF16), pltpu.VMEM((n // tn, 1, tn), F32)],
        compiler_params=_params(("arbitrary", "arbitrary")),
        name="norm_matmul_shift" if shift else "norm_matmul",
    )(x, gain.reshape(1, d), w_bf16, mu.reshape(1, n))


def _mm_res_kernel(a_ref, w_ref, x_ref, o_ref):
    o_ref[...] = x_ref[...] + _dot(a_ref[...], w_ref[...])


def _matmul_residual(a_bf16, w_bf16, x, *, tm, tn):
    m, k = a_bf16.shape
    n = w_bf16.shape[1]
    assert m % tm == 0 and n % tn == 0
    return pl.pallas_call(
        _mm_res_kernel,
        grid=(m // tm, n // tn),
        in_specs=[
            pl.BlockSpec((tm, k), lambda i, j: (i, 0)),
            pl.BlockSpec((k, tn), lambda i, j: (0, j)),
            pl.BlockSpec((tm, tn), lambda i, j: (i, j)),
        ],
        out_specs=pl.BlockSpec((tm, tn), lambda i, j: (i, j)),
        out_shape=jax.ShapeDtypeStruct((m, n), F32),
        compiler_params=_params(("parallel", "arbitrary")),
        name="out_proj_residual",
    )(a_bf16, w_bf16, x)


def _rope_kernel(pos_ref, inv_ref, cos_ref, sin_ref):
    ang = pos_ref[...].astype(F32) * inv_ref[...]
    cos_ref[...] = jnp.cos(ang)
    sin_ref[...] = jnp.sin(ang)


def _rope_tables(positions_flat, *, tr):
    m = positions_flat.shape[0]
    inv = ROPE_THETA ** (-jnp.arange(0, DA_HEAD_DIM, 2, dtype=F32) / DA_HEAD_DIM)
    inv = jnp.concatenate([inv, inv]).reshape(1, DA_HEAD_DIM)
    return pl.pallas_call(
        _rope_kernel,
        grid=(m // tr,),
        in_specs=[pl.BlockSpec((tr, 1), lambda i: (i, 0)), pl.BlockSpec((1, DA_HEAD_DIM), lambda i: (0, 0))],
        out_specs=[pl.BlockSpec((tr, DA_HEAD_DIM), lambda i: (i, 0))] * 2,
        out_shape=[jax.ShapeDtypeStruct((m, DA_HEAD_DIM), F32)] * 2,
        compiler_params=_params(("parallel",)),
        name="rope_tables",
    )(positions_flat.reshape(m, 1), inv)


def _qkv_prep_kernel(q_ref, k_ref, v_ref, cos_ref, sin_ref, qg_ref, kg_ref, qo_ref, ko_ref, vo_ref, *, scale):
    cos = cos_ref[...]
    lane = lax.broadcasted_iota(jnp.int32, cos.shape, 1)
    sin_signed = jnp.where(lane < DA_HEAD_DIM // 2, -sin_ref[...], sin_ref[...])
    width = q_ref.shape[1]

    def prep(src, gain, dst, mult):
        for c in range(width // DA_HEAD_DIM):
            sl = slice(c * DA_HEAD_DIM, (c + 1) * DA_HEAD_DIM)
            x = src[:, sl]
            y = x * lax.rsqrt(jnp.mean(x * x, axis=-1, keepdims=True) + NORM_EPS) * gain
            rot = pltpu.roll(y, DA_HEAD_DIM // 2, axis=1)
            dst[:, sl] = ((y * cos + rot * sin_signed) * mult).astype(BF16)

    prep(q_ref, qg_ref[...], qo_ref, scale)
    prep(k_ref, kg_ref[...], ko_ref, 1.0)
    vo_ref[...] = v_ref[...].astype(BF16)


def _qkv_prep(proj, cos, sin, q_gain, k_gain, *, width, tr):
    m = proj.shape[0]
    blk = lambda c: pl.BlockSpec((tr, width), lambda i: (i, c))
    rope = pl.BlockSpec((tr, DA_HEAD_DIM), lambda i: (i, 0))
    gain = pl.BlockSpec((1, DA_HEAD_DIM), lambda i: (0, 0))
    kern = functools.partial(_qkv_prep_kernel, scale=DA_HEAD_DIM ** -0.5)
    return pl.pallas_call(
        kern,
        grid=(m // tr,),
        in_specs=[blk(0), blk(1), blk(2), rope, rope, gain, gain],
        out_specs=[pl.BlockSpec((tr, width), lambda i: (i, 0))] * 3,
        out_shape=[jax.ShapeDtypeStruct((m, width), BF16)] * 3,
        compiler_params=_params(("parallel",)),
        name="qkv_prep",
    )(proj, proj, proj, cos, sin, q_gain.reshape(1, -1), k_gain.reshape(1, -1))


def _flash_kernel(qtab_ref, ktab_ref, q_ref, k_ref, v_ref, gate_ref, lq1_ref, lk1_ref, lq2_ref, lk2_ref, sub_ref,
                  o_ref, m_ref, l_ref, acc_ref, *, lambda_init):
    step_id = pl.program_id(2)
    qi = qtab_ref[step_id]
    ki = ktab_ref[step_id]
    tq = q_ref.shape[0]
    tk = k_ref.shape[0]
    reps = tk // LANES

    @pl.when(ki == 0)
    def _():
        m_ref[...] = jnp.full(m_ref.shape, NEG_BIG, F32)
        l_ref[...] = jnp.zeros(l_ref.shape, F32)
        acc_ref[...] = jnp.zeros(acc_ref.shape, F32)

    def step(masked):
        v = v_ref[...]
        comps = range(2)
        sls = [slice(c * DA_HEAD_DIM, (c + 1) * DA_HEAD_DIM) for c in comps]
        s = [_dot_nt(q_ref[:, sl], k_ref[:, sl]) for sl in sls]
        if masked:
            row = lax.broadcasted_iota(jnp.int32, (tq, tk), 0)
            col = lax.broadcasted_iota(jnp.int32, (tq, tk), 1)
            keep = col <= row
            s = [jnp.where(keep, x, NEG_BIG) for x in s]
        m_prev = [m_ref[c] for c in comps]
        m_new = [jnp.maximum(mp, jnp.max(x, axis=-1, keepdims=True)) for mp, x in zip(m_prev, s)]
        alpha = [jnp.exp(mp - mn) for mp, mn in zip(m_prev, m_new)]
        p = [jnp.exp(x - jnp.concatenate([mn] * reps, axis=1)) for x, mn in zip(s, m_new)]
        pv = [_dot(x.astype(BF16), v) for x in p]
        for c in comps:
            part = p[c][:, 0:LANES]
            for j in range(1, reps):
                part = part + p[c][:, j * LANES:(j + 1) * LANES]
            l_ref[c] = alpha[c] * l_ref[c] + part
            acc_ref[c] = jnp.concatenate([alpha[c]] * (DA_VALUE_DIM // LANES), axis=1) * acc_ref[c] + pv[c]
            m_ref[c] = m_new[c]

    @pl.when(ki < qi)
    def _():
        step(False)

    @pl.when(ki == qi)
    def _():
        step(True)
        lam = (jnp.exp(jnp.sum(lq1_ref[...] * lk1_ref[...], axis=-1, keepdims=True))
               - jnp.exp(jnp.sum(lq2_ref[...] * lk2_ref[...], axis=-1, keepdims=True)) + lambda_init)
        l0 = jnp.sum(l_ref[0], axis=-1, keepdims=True)
        l1 = jnp.sum(l_ref[1], axis=-1, keepdims=True)
        o = acc_ref[0] / l0 - lam * (acc_ref[1] / l1)
        o = o * lax.rsqrt(jnp.mean(o * o, axis=-1, keepdims=True) + NORM_EPS) * sub_ref[...]
        g = gate_ref[...]
        o_ref[...] = (o * (1.0 - lambda_init) * (g * _sigmoid(g))).astype(BF16)


def _flash_diff_attention(q, k, v, proj, lam_q1, lam_k1, lam_q2, lam_k2, subln_w, *, batch, seq, heads,
                          lambda_init, tq):
    m = q.shape[0]
    nq = seq // tq
    gate_col0 = 3 * heads
    pairs = [(a, b) for a in range(nq) for b in range(a + 1)]
    qtab = jnp.asarray([a for a, _ in pairs], jnp.int32)
    ktab = jnp.asarray([b for _, b in pairs], jnp.int32)
    row_q = lambda b, h, s, qt, kt: (b * nq + qt[s], h)
    row_k = lambda b, h, s, qt, kt: (b * nq + kt[s], h)
    vec = lambda n: pl.BlockSpec((1, n), lambda b, h, s, qt, kt: (0, 0))
    kern = functools.partial(_flash_kernel, lambda_init=lambda_init)
    grid_spec = pltpu.PrefetchScalarGridSpec(
        num_scalar_prefetch=2,
        grid=(batch, heads, len(pairs)),
        in_specs=[
            pl.BlockSpec((tq, DA_VALUE_DIM), row_q),
            pl.BlockSpec((tq, DA_VALUE_DIM), row_k),
            pl.BlockSpec((tq, DA_VALUE_DIM), row_k),
            pl.BlockSpec((tq, DA_VALUE_DIM), lambda b, h, s, qt, kt: (b * nq + qt[s], gate_col0 + h)),
            vec(DA_HEAD_DIM), vec(DA_HEAD_DIM), vec(DA_HEAD_DIM), vec(DA_HEAD_DIM), vec(DA_VALUE_DIM),
        ],
        out_specs=pl.BlockSpec((tq, DA_VALUE_DIM), row_q),
        scratch_shapes=[pltpu.VMEM((2, tq, LANES), F32), pltpu.VMEM((2, tq, LANES), F32),
                        pltpu.VMEM((2, tq, DA_VALUE_DIM), F32)],
    )
    return pl.pallas_call(
        kern,
        grid_spec=grid_spec,
        out_shape=jax.ShapeDtypeStruct((m, heads * DA_VALUE_DIM), BF16),
        compiler_params=_params(("parallel", "parallel", "arbitrary")),
        name="flash_diff_attention",
    )(qtab, ktab, q, k, v, proj, lam_q1.reshape(1, -1), lam_k1.reshape(1, -1), lam_q2.reshape(1, -1),
      lam_k2.reshape(1, -1), subln_w.reshape(1, -1))


def _rwkv_prep_kernel(r_ref, k_ref, v_ref, lora_ref, vfirst_ref, w0_ref, a0_ref, kk_ref, ka_ref, rk_ref, v0_ref,
                      dup_ref, iup_ref, vup_ref,
                      at_ref, bt_ref, kt_ref, rt_ref, vo_ref, bonus_ref, gend_ref, *, vres):
    ones_blk = _half_ones()
    ch = WKV_CHUNK
    lora = lora_ref[...]
    dwt = jnp.tanh(lora[:, 0:LORA_PAD]).astype(BF16)
    da = lora[:, LORA_PAD:2 * LORA_PAD].astype(BF16)
    z = -(w0_ref[...] + _dot(dwt, dup_ref[...]))
    w_log = -(jnp.maximum(z, 0.0) + jnp.log1p(jnp.exp(-jnp.abs(z)))) - 0.5
    lw = -jnp.exp(w_log)
    a = _sigmoid(a0_ref[...] + _dot(da, iup_ref[...]))
    r = r_ref[...]
    k = k_ref[...]
    v = v_ref[...]
    if vres:
        pv = lora[:, 2 * LORA_PAD:3 * LORA_PAD].astype(BF16)
        v = v + (vfirst_ref[...] - v) * _sigmoid(v0_ref[...] + _dot(pv, vup_ref[...]))
    vo_ref[...] = v
    kk = k * kk_ref[...]
    k_mod = k * (1.0 + (a - 1.0) * ka_ref[...])
    rkr = r * k_mod * rk_ref[...]

    ri = lax.broadcasted_iota(jnp.int32, (ch, ch), 0)
    ci = lax.broadcasted_iota(jnp.int32, (ch, ch), 1)
    tri = jnp.where(ci <= ri, 1.0, 0.0).astype(BF16)
    lw_hi, lw_mid, lw_lo = _split3(lw)
    for t in range(r.shape[0] // ch):
        rows = slice(t * ch, (t + 1) * ch)
        cum = _dot(tri, lw_hi[rows]) + (_dot(tri, lw_mid[rows]) + _dot(tri, lw_lo[rows]))
        g_in = jnp.exp(cum)
        g_inv = jnp.exp(-cum)
        g_ex = jnp.exp(cum - lw[rows])
        gend_ref[t] = g_in[ch - 1:ch, :]
        rt_ref[rows, :] = r[rows] * g_in
        kt_ref[rows, :] = k_mod[rows] * g_inv
        for c in range(r.shape[1] // LANES):
            sl = slice(c * LANES, (c + 1) * LANES)
            kc = kk[rows, sl]
            nrm = jnp.maximum(jnp.sqrt(_segsum(kc * kc, ones_blk)), 1e-12)
            kn = kc / nrm
            at_ref[rows, sl] = -kn * g_ex[:, sl]
            bt_ref[rows, sl] = kn * a[rows, sl] * g_inv[:, sl]
            bonus_ref[rows, sl] = _segsum(rkr[rows, sl], ones_blk) * v[rows, sl]


def _rwkv_prep(pm, lora, v_first, w0, a0, k_k, k_a, r_k, v0, dup, iup, vup, *, width, vres, tr, tc):
    m = pm.shape[0]
    ncb = width // tc
    cpt = tr // WKV_CHUNK
    col = lambda c0: pl.BlockSpec((tr, tc), lambda i, j: (i, c0 * ncb + j))
    vec = pl.BlockSpec((1, tc), lambda i, j: (0, j))
    up = pl.BlockSpec((LORA_PAD, tc), lambda i, j: (0, j))
    kern = functools.partial(_rwkv_prep_kernel, vres=vres)
    if v_first is None:
        v_first, vf_spec = pm, col(2)
    else:
        vf_spec = col(0)
    row = lambda a: a.reshape(1, width)
    return pl.pallas_call(
        kern,
        grid=(m // tr, ncb),
        in_specs=[col(0), col(1), col(2), pl.BlockSpec((tr, lora.shape[1]), lambda i, j: (i, 0)), vf_spec,
                  vec, vec, vec, vec, vec, vec, up, up, up],
        out_specs=[pl.BlockSpec((tr, tc), lambda i, j: (i, j))] * 6
                  + [pl.BlockSpec((cpt, 1, tc), lambda i, j: (i, 0, j))],
        out_shape=[jax.ShapeDtypeStruct((m, width), F32)] * 6
                  + [jax.ShapeDtypeStruct((m // WKV_CHUNK, 1, width), F32)],
        compiler_params=_params(("parallel", "parallel")),
        name="rwkv_prep",
    )(pm, pm, pm, lora, v_first, row(w0), row(a0), row(k_k), row(k_a), row(r_k), row(v0), dup, iup, vup)


def _mm(a, b, nt=False):
    return _dot1(a, b, nt) if WKV_PASSES == 1 else _dot3(a, b, nt)


def _unit_lower_inverse(lows, row, col):
    blk = lambda sh: (row >> sh) == (col >> sh)
    eye = jnp.where(row == col, 1.0, 0.0)
    d1 = [jnp.where(blk(3), lo, 0.0) for lo in lows]
    d2 = [_mm(d, d) for d in d1]
    d4 = [_mm(d, d) for d in d2]
    xs = [eye + a + b + _mm(a, b) for a, b in zip(d1, d2)]
    xs = [x + _mm(x, d) for x, d in zip(xs, d4)]
    for sh in (3, 4, 5):
        off = jnp.logical_and(blk(sh + 1), jnp.logical_not(blk(sh)))
        ns = [jnp.where(off, lo, 0.0) for lo in lows]
        ts = [_mm(x, n) for x, n in zip(xs, ns)]
        xs = [x + _mm(t, x) for x, t in zip(xs, ts)]
    return xs


def _wkv_kernel(at_ref, bt_ref, kt_ref, rt_ref, v_ref, gend_ref, bonus_ref, g_ref, gnw_ref, gnb_ref, o_ref, s_ref):
    c = pl.program_id(2)
    units = range(at_ref.shape[1] // LANES)
    ch = WKV_CHUNK
    n2 = 2 * ch

    @pl.when(c == 0)
    def _():
        s_ref[...] = jnp.zeros(s_ref.shape, F32)

    row = lax.broadcasted_iota(jnp.int32, (n2, n2), 0)
    col = lax.broadcasted_iota(jnp.int32, (n2, n2), 1)
    strict = col < row
    incl = col <= row
    lane = lax.broadcasted_iota(jnp.int32, (ch, LANES), 1)
    m0 = jnp.where(lane < RW_HEAD_DIM, 1.0, 0.0)
    m1 = 1.0 - m0
    ones_blk = _half_ones()
    sls = [slice(u * LANES, (u + 1) * LANES) for u in units]

    def stack2(ref):
        out = []
        for sl in sls:
            x = ref[:, sl]
            out.append(jnp.concatenate([x * m0, x * m1], axis=0))
        return out

    a2, b2, k2, r2, v2 = stack2(at_ref), stack2(bt_ref), stack2(kt_ref), stack2(rt_ref), stack2(v_ref)
    bk = [jnp.concatenate([b, k], axis=0) for b, k in zip(b2, k2)]
    sc = [_mm(jnp.concatenate([a, r], axis=0), x, nt=True) for a, r, x in zip(a2, r2, bk)]
    low_ab = [jnp.where(strict, s[:n2, :n2], 0.0) for s in sc]
    low_ak = [jnp.where(strict, s[:n2, n2:], 0.0) for s in sc]
    low_rb = [jnp.where(incl, s[n2:, :n2], 0.0) for s in sc]
    low_rk = [jnp.where(incl, s[n2:, n2:], 0.0) for s in sc]

    w1 = [_mm(l, v) for l, v in zip(low_ak, v2)]
    rkv = [_mm(l, v) for l, v in zip(low_rk, v2)]
    tinv = _unit_lower_inverse(low_ab, row, col)
    au = [_mm(t, jnp.concatenate([a, w], axis=1)) for t, a, w in zip(tinv, a2, w1)]
    ry = [_mm(l, x) for l, x in zip(low_rb, au)]

    ss = [s_ref[u] for u in units]
    y2 = [_mm(r + x[:, :n2], s, nt=True) + (x[:, n2:] + z) for r, x, s, z in zip(r2, ry, ss, rkv)]
    ps = [_mm(x[:, :n2].T, b) for x, b in zip(au, b2)]
    qs = [_mm(jnp.concatenate([x[:, n2:], v], axis=0).T, y) for x, v, y in zip(au, v2, bk)]
    sp = [_mm(s, p) for s, p in zip(ss, ps)]
    for u in units:
        s_ref[u] = (ss[u] + sp[u] + qs[u]) * gend_ref[0, :, sls[u]]

    inv_n = 1.0 / RW_HEAD_DIM
    ys = [y[:ch] + y[ch:] for y in y2]
    ds = [y - _segsum(y, ones_blk) * inv_n for y in ys]
    var = [_segsum(d * d, ones_blk) * inv_n for d in ds]
    for u in units:
        sl = sls[u]
        yn = ds[u] * lax.rsqrt(var[u] + GN_EPS) * gnw_ref[:, sl] + gnb_ref[:, sl]
        g = g_ref[:, sl]
        o_ref[:, sl] = ((yn + bonus_ref[:, sl]) * (g * _sigmoid(g))).astype(BF16)


def _wkv(pm, at, bt, kt, rt, v, gend, bonus, gn_w, gn_b, *, batch, seq, width, units):
    m = pm.shape[0]
    nc = seq // WKV_CHUNK
    tc = units * LANES
    ncb = width // tc
    blk = pl.BlockSpec((WKV_CHUNK, tc), lambda b, j, c: (b * nc + c, j))
    gate = pl.BlockSpec((WKV_CHUNK, tc), lambda b, j, c: (b * nc + c, 3 * ncb + j))
    vec = pl.BlockSpec((1, tc), lambda b, j, c: (0, j))
    return pl.pallas_call(
        _wkv_kernel,
        grid=(batch, ncb, nc),
        in_specs=[blk, blk, blk, blk, blk, pl.BlockSpec((1, 1, tc), lambda b, j, c: (b * nc + c, 0, j)),
                  blk, gate, vec, vec],
        out_specs=blk,
        out_shape=jax.ShapeDtypeStruct((m, width), BF16),
        scratch_shapes=[pltpu.VMEM((units, 2 * WKV_CHUNK, 2 * WKV_CHUNK), F32)],
        compiler_params=_params(("parallel", "parallel", "arbitrary")),
        name="wkv7_chunked",
    )(at, bt, kt, rt, v, gend, bonus, pm, gn_w.reshape(1, width), gn_b.reshape(1, width))


def _tiles(m, seq):
    tm = min(512, seq)
    return dict(tm=tm, tn=min(1024, 4 * DA_VALUE_DIM), tr=min(256, seq), tq=min(512, seq))


def _attention_layer(x, cos, sin, norm_w, w_in, q_gain, k_gain, lq1, lk1, lq2, lk2, subln_w, w_out, lambda_init,
                     *, batch, seq, t):
    d = x.shape[1]
    heads = d // DA_VALUE_DIM
    proj = _norm_matmul(x, norm_w, w_in.astype(BF16), jnp.zeros((w_in.shape[1],), F32), shift=False,
                        rows_per_seq=seq, tm=t["tm"], tn=t["tn"])
    q, k, v = _qkv_prep(proj, cos, sin, q_gain, k_gain, width=d, tr=t["tr"])
    o = _flash_diff_attention(q, k, v, proj, lq1, lk1, lq2, lk2, subln_w, batch=batch, seq=seq, heads=heads,
                              lambda_init=lambda_init, tq=t["tq"])
    return _matmul_residual(o, w_out.astype(BF16), x, tm=t["tm"], tn=min(t["tn"], d))


def _pad_rows(w, rows):
    return jnp.pad(w, ((0, rows - w.shape[0]), (0, 0)))


def _pad_cols(w, cols):
    return jnp.pad(w, ((0, 0), (0, cols - w.shape[1])))


def _rwkv_layer(x, v_first, norm_w, w_in, mu, w0, decay_up, a0, iclr_up, k_k, k_a, r_k, gn_w, gn_b, w_out, vres,
                *, batch, seq, t):
    d = x.shape[1]
    wide = 4 * d
    dr = decay_up.shape[0]
    ir = iclr_up.shape[0]
    lora_w = [_pad_cols(w_in[:, wide:wide + dr], LORA_PAD), _pad_cols(w_in[:, wide + dr:wide + dr + ir], LORA_PAD)]
    lora_mu = [jnp.pad(mu[wide:wide + dr], (0, LORA_PAD - dr)), jnp.pad(mu[wide + dr:wide + dr + ir], (0, LORA_PAD - ir))]
    if vres is not None:
        vd_w, vd_mu, v0, vu_w = vres
        lora_w.append(_pad_cols(vd_w, LORA_PAD))
        lora_mu.append(jnp.pad(vd_mu, (0, LORA_PAD - vd_mu.shape[0])))
        vup = _pad_rows(vu_w, LORA_PAD).astype(BF16)
    else:
        v0 = jnp.zeros((d,), F32)
        vup = jnp.zeros((LORA_PAD, d), BF16)
    lora_w = jnp.concatenate(lora_w, axis=1).astype(BF16)
    lora_mu = jnp.concatenate(lora_mu)
    pm = _norm_matmul(x, norm_w, w_in[:, :wide].astype(BF16), mu[:wide], shift=True, rows_per_seq=seq,
                      tm=t["tm"], tn=t["tn"])
    lora = _norm_matmul(x, norm_w, lora_w, lora_mu, shift=True, rows_per_seq=seq, tm=t["tm"], tn=lora_w.shape[1])
    at, bt, kt, rt, v, bonus, gend = _rwkv_prep(
        pm, lora, v_first, w0, a0, k_k, k_a, r_k.reshape(-1), v0,
        _pad_rows(decay_up, LORA_PAD).astype(BF16), _pad_rows(iclr_up, LORA_PAD).astype(BF16), vup,
        width=d, vres=vres is not None, tr=t["tm"], tc=min(512, d))
    o = _wkv(pm, at, bt, kt, rt, v, gend, bonus, gn_w, gn_b, batch=batch, seq=seq, width=d,
             units=min(WKV_UNITS, d // LANES))
    return _matmul_residual(o, w_out.astype(BF16), x, tm=t["tm"], tn=min(t["tn"], d)), v


def kernel(x, positions, norm_w, da_w_in, da_q_gain, da_k_gain, da_lam_q1, da_lam_k1, da_lam_q2, da_lam_k2, da_subln_w, da_w_out, rw_w_in, rw_mu, rw_w0, rw_decay_up, rw_a0, rw_iclr_up, rw_k_k, rw_k_a, rw_r_k, rw_gn_w, rw_gn_b, rw_w_out, rw_vres_down, rw_vres_mu, rw_v0, rw_vres_up):
    batch, seq, d = x.shape
    depth = norm_w.shape[0]
    m = batch * seq
    t = _tiles(m, seq)
    xf = x.reshape(m, d)
    cos, sin = _rope_tables(positions.reshape(m), tr=t["tr"])
    v_first = None
    for i in range(depth):
        j = i // 2
        if i % 2 == 0:
            lambda_init = 0.8 - 0.6 * math.exp(-0.3 * i)
            xf = _attention_layer(xf, cos, sin, norm_w[i], da_w_in[j], da_q_gain[j], da_k_gain[j], da_lam_q1[j],
                                  da_lam_k1[j], da_lam_q2[j], da_lam_k2[j], da_subln_w[j], da_w_out[j], lambda_init,
                                  batch=batch, seq=seq, t=t)
        else:
            vres = None if j == 0 else (rw_vres_down[j - 1], rw_vres_mu[j - 1], rw_v0[j - 1], rw_vres_up[j - 1])
            xf, v_now = _rwkv_layer(xf, v_first, norm_w[i], rw_w_in[j], rw_mu[j], rw_w0[j], rw_decay_up[j], rw_a0[j],
                                    rw_iclr_up[j], rw_k_k[j], rw_k_a[j], rw_r_k[j], rw_gn_w[j], rw_gn_b[j],
                                    rw_w_out[j], vres, batch=batch, seq=seq, t=t)
            if v_first is None:
                v_first = v_now
    return xf.reshape(batch, seq, d)
```

```python
import functools
import math

import jax
import jax.numpy as jnp
from jax import lax
from jax.experimental import pallas as pl
from jax.experimental.pallas import tpu as pltpu

F32 = jnp.float32
BF16 = jnp.bfloat16

LANES = 128
DA_HEAD_DIM = 128
DA_VALUE_DIM = 2 * DA_HEAD_DIM
RW_HEAD_DIM = 64
WKV_CHUNK = 64
WKV_UNITS = 8
LORA_PAD = 128
ROPE_THETA = 10000.0
NORM_EPS = 1e-6
GN_EPS = 64e-5
NEG_BIG = -1e30
VMEM_LIMIT = 56 * 1024 * 1024


def _params(sem):
    return pltpu.CompilerParams(dimension_semantics=sem, vmem_limit_bytes=VMEM_LIMIT)


def _dot(a, b):
    return jnp.dot(a, b, preferred_element_type=F32)


def _dot_nt(a, b):
    return lax.dot_general(a, b, (((1,), (1,)), ((), ())), preferred_element_type=F32)


def _split2(x):
    hi = x.astype(BF16)
    lo = (x - hi.astype(F32)).astype(BF16)
    return hi, lo


def _mm(a, b, nt=False):
    d = _dot_nt if nt else _dot
    return d(a.astype(BF16), b.astype(BF16))


def _segsum(x, ones_blk):
    hi, lo = _split2(x)
    return _dot(hi, ones_blk) + _dot(lo, ones_blk)


def _half_ones():
    r = lax.broadcasted_iota(jnp.int32, (LANES, LANES), 0)
    c = lax.broadcasted_iota(jnp.int32, (LANES, LANES), 1)
    return jnp.where((r >= RW_HEAD_DIM) == (c >= RW_HEAD_DIM), 1.0, 0.0).astype(BF16)


def _sigmoid(x):
    return 1.0 / (1.0 + jnp.exp(-x))


def _store_normed(x_ref, g_ref, xn_ref):
    x = x_ref[...]
    ms = jnp.mean(x * x, axis=-1, keepdims=True)
    xn_ref[...] = (x * lax.rsqrt(ms + NORM_EPS) * g_ref[...]).astype(BF16)


def _rwkv_in_kernel(x_ref, g_ref, w_ref, mu_ref, o_ref, xn_ref, prev_ref, *, rows_per_seq):
    i = pl.program_id(0)
    j = pl.program_id(1)
    tm = x_ref.shape[0]

    @pl.when(j == 0)
    def _():
        _store_normed(x_ref, g_ref, xn_ref)

    @pl.when(i == 0)
    def _():
        prev_ref[j] = jnp.zeros(prev_ref.shape[1:], F32)

    acc = _dot(xn_ref[...], w_ref[...])
    seq_start = (i * tm) % rows_per_seq == 0
    prev = jnp.where(seq_start, 0.0, prev_ref[j])
    row = lax.broadcasted_iota(jnp.int32, acc.shape, 0)
    shifted = jnp.where(row == 0, prev, pltpu.roll(acc, 1, axis=0))
    prev_ref[j] = acc[tm - 1:tm, :]
    o_ref[...] = acc + (shifted - acc) * mu_ref[...]


def _rwkv_in_proj(x, gain, w_bf16, mu, *, rows_per_seq, tm, tn):
    m, d = x.shape
    n = w_bf16.shape[1]
    assert m % tm == 0 and n % tn == 0 and rows_per_seq % tm == 0
    return pl.pallas_call(
        functools.partial(_rwkv_in_kernel, rows_per_seq=rows_per_seq),
        grid=(m // tm, n // tn),
        in_specs=[
            pl.BlockSpec((tm, d), lambda i, j: (i, 0)),
            pl.BlockSpec((1, d), lambda i, j: (0, 0)),
            pl.BlockSpec((d, tn), lambda i, j: (0, j)),
            pl.BlockSpec((1, tn), lambda i, j: (0, j)),
        ],
        out_specs=pl.BlockSpec((tm, tn), lambda i, j: (i, j)),
        out_shape=jax.ShapeDtypeStruct((m, n), F32),
        scratch_shapes=[pltpu.VMEM((tm, d), BF16), pltpu.VMEM((n // tn, 1, tn), F32)],
        compiler_params=_params(("arbitrary", "arbitrary")),
        name="rwkv_in_proj",
    )(x, gain.reshape(1, d), w_bf16, mu.reshape(1, n))


def _attn_in_kernel(x_ref, g_ref, w_ref, cos_ref, sin_ref, qg_ref, kg_ref, o_ref, xn_ref, *, sec, scale):
    j = pl.program_id(1)

    @pl.when(j == 0)
    def _():
        _store_normed(x_ref, g_ref, xn_ref)

    acc = _dot(xn_ref[...], w_ref[...])

    @pl.when(j < 2 * sec)
    def _():
        is_q = j < sec
        gain = jnp.where(is_q, qg_ref[...], kg_ref[...])
        mult = jnp.where(is_q, scale, 1.0)
        cos = cos_ref[...]
        lane = lax.broadcasted_iota(jnp.int32, cos.shape, 1)
        sin_signed = jnp.where(lane < DA_HEAD_DIM // 2, -sin_ref[...], sin_ref[...])
        for c in range(acc.shape[1] // DA_HEAD_DIM):
            sl = slice(c * DA_HEAD_DIM, (c + 1) * DA_HEAD_DIM)
            x = acc[:, sl]
            y = x * lax.rsqrt(jnp.mean(x * x, axis=-1, keepdims=True) + NORM_EPS) * gain
            rot = pltpu.roll(y, DA_HEAD_DIM // 2, axis=1)
            o_ref[:, sl] = ((y * cos + rot * sin_signed) * mult).astype(BF16)

    @pl.when(j >= 2 * sec)
    def _():
        o_ref[...] = acc.astype(BF16)


def _attn_in_proj(x, gain, w_bf16, cos, sin, q_gain, k_gain, *, tm, tn):
    m, d = x.shape
    n = w_bf16.shape[1]
    assert m % tm == 0 and d % tn == 0 and n == 4 * d
    rope = pl.BlockSpec((tm, DA_HEAD_DIM), lambda i, j: (i, 0))
    vec = pl.BlockSpec((1, DA_HEAD_DIM), lambda i, j: (0, 0))
    return pl.pallas_call(
        functools.partial(_attn_in_kernel, sec=d // tn, scale=DA_HEAD_DIM ** -0.5),
        grid=(m // tm, n // tn),
        in_specs=[
            pl.BlockSpec((tm, d), lambda i, j: (i, 0)),
            pl.BlockSpec((1, d), lambda i, j: (0, 0)),
            pl.BlockSpec((d, tn), lambda i, j: (0, j)),
            rope, rope, vec, vec,
        ],
        out_specs=pl.BlockSpec((tm, tn), lambda i, j: (i, j)),
        out_shape=jax.ShapeDtypeStruct((m, n), BF16),
        scratch_shapes=[pltpu.VMEM((tm, d), BF16)],
        compiler_params=_params(("parallel", "arbitrary")),
        name="attn_in_proj",
    )(x, gain.reshape(1, d), w_bf16, cos, sin, q_gain.reshape(1, -1), k_gain.reshape(1, -1))


def _mm_res_kernel(a_ref, w_ref, x_ref, o_ref):
    o_ref[...] = x_ref[...] + _dot(a_ref[...], w_ref[...])


def _matmul_residual(a_bf16, w_bf16, x, *, tm, tn):
    m, k = a_bf16.shape
    n = w_bf16.shape[1]
    assert m % tm == 0 and n % tn == 0
    return pl.pallas_call(
        _mm_res_kernel,
        grid=(m // tm, n // tn),
        in_specs=[
            pl.BlockSpec((tm, k), lambda i, j: (i, 0)),
            pl.BlockSpec((k, tn), lambda i, j: (0, j)),
            pl.BlockSpec((tm, tn), lambda i, j: (i, j)),
        ],
        out_specs=pl.BlockSpec((tm, tn), lambda i, j: (i, j)),
        out_shape=jax.ShapeDtypeStruct((m, n), F32),
        compiler_params=_params(("parallel", "arbitrary")),
        name="out_proj_residual",
    )(a_bf16, w_bf16, x)


def _rope_kernel(pos_ref, inv_ref, cos_ref, sin_ref):
    ang = pos_ref[...].astype(F32) * inv_ref[...]
    cos_ref[...] = jnp.cos(ang)
    sin_ref[...] = jnp.sin(ang)


def _rope_tables(positions_flat, *, tr):
    m = positions_flat.shape[0]
    inv = ROPE_THETA ** (-jnp.arange(0, DA_HEAD_DIM, 2, dtype=F32) / DA_HEAD_DIM)
    inv = jnp.concatenate([inv, inv]).reshape(1, DA_HEAD_DIM)
    return pl.pallas_call(
        _rope_kernel,
        grid=(m // tr,),
        in_specs=[pl.BlockSpec((tr, 1), lambda i: (i, 0)), pl.BlockSpec((1, DA_HEAD_DIM), lambda i: (0, 0))],
        out_specs=[pl.BlockSpec((tr, DA_HEAD_DIM), lambda i: (i, 0))] * 2,
        out_shape=[jax.ShapeDtypeStruct((m, DA_HEAD_DIM), F32)] * 2,
        compiler_params=_params(("parallel",)),
        name="rope_tables",
    )(positions_flat.reshape(m, 1), inv)


def _flash_kernel(qtab_ref, ktab_ref, kind_ref, q_ref, k_ref, v_ref, gate_ref, lq1_ref, lk1_ref, lq2_ref, lk2_ref,
                  sub_ref, o_ref, m_ref, l_ref, acc_ref, *, lambda_init):
    step_id = pl.program_id(2)
    kb = ktab_ref[step_id]
    kind = kind_ref[step_id]
    tq = q_ref.shape[0]
    reps = tq // LANES
    comps = range(2)
    sls = [slice(c * DA_HEAD_DIM, (c + 1) * DA_HEAD_DIM) for c in comps]

    @pl.when(kb == 0)
    def _():
        m_ref[...] = jnp.full(m_ref.shape, NEG_BIG, F32)
        l_ref[...] = jnp.zeros(l_ref.shape, F32)
        acc_ref[...] = jnp.zeros(acc_ref.shape, F32)

    def process(masks):
        rows = [slice(g * tq, (g + 1) * tq) for g in range(len(masks))]
        s = [[_dot_nt(q_ref[:, sl], k_ref[r, sl]) for sl in sls] for r in rows]
        m = [m_ref[c] for c in comps]
        l = [l_ref[c] for c in comps]
        acc = [acc_ref[c] for c in comps]
        for g, masked in enumerate(masks):
            v = v_ref[rows[g], :]
            for c in comps:
                x = s[g][c]
                if masked:
                    row = lax.broadcasted_iota(jnp.int32, (tq, tq), 0)
                    col = lax.broadcasted_iota(jnp.int32, (tq, tq), 1)
                    x = jnp.where(col <= row, x, NEG_BIG)
                m_new = jnp.maximum(m[c], jnp.max(x, axis=-1, keepdims=True))
                alpha = jnp.exp(m[c] - m_new)
                p = jnp.exp(x - jnp.concatenate([m_new] * reps, axis=1))
                part = p[:, 0:LANES]
                for j in range(1, reps):
                    part = part + p[:, j * LANES:(j + 1) * LANES]
                l[c] = alpha * l[c] + part
                acc[c] = jnp.concatenate([alpha] * (DA_VALUE_DIM // LANES), axis=1) * acc[c] + _dot(p.astype(BF16), v)
                m[c] = m_new
        return m, l, acc

    def carry(m, l, acc):
        for c in comps:
            m_ref[c] = m[c]
            l_ref[c] = l[c]
            acc_ref[c] = acc[c]

    def finish(l, acc):
        lam = (jnp.exp(jnp.sum(lq1_ref[...] * lk1_ref[...], axis=-1, keepdims=True))
               - jnp.exp(jnp.sum(lq2_ref[...] * lk2_ref[...], axis=-1, keepdims=True)) + lambda_init)
        l0 = jnp.sum(l[0], axis=-1, keepdims=True)
        l1 = jnp.sum(l[1], axis=-1, keepdims=True)
        o = acc[0] / l0 - lam * (acc[1] / l1)
        o = o * lax.rsqrt(jnp.mean(o * o, axis=-1, keepdims=True) + NORM_EPS) * sub_ref[...]
        g = gate_ref[...].astype(F32)
        o_ref[...] = (o * (1.0 - lambda_init) * (g * _sigmoid(g))).astype(BF16)

    @pl.when(kind == 0)
    def _():
        carry(*process([False, False]))

    @pl.when(kind == 1)
    def _():
        _, l, acc = process([True])
        finish(l, acc)

    @pl.when(kind == 2)
    def _():
        _, l, acc = process([False, True])
        finish(l, acc)


def _flash_diff_attention(proj, lam_q1, lam_k1, lam_q2, lam_k2, subln_w, *, batch, seq, heads,
                          lambda_init, tq):
    m = proj.shape[0]
    nq = seq // tq
    nkb = seq // (2 * tq)
    assert seq % (2 * tq) == 0
    steps = [(a, b, 0 if b < a // 2 else 1 + a % 2) for a in range(nq) for b in range(a // 2 + 1)]
    qtab, ktab, kinds = (jnp.asarray(col, jnp.int32) for col in zip(*steps))
    q_blk = lambda c0: pl.BlockSpec((tq, DA_VALUE_DIM),
                                    lambda b, h, s, qt, kt, kd: (b * nq + qt[s], c0 * heads + h))
    k_blk = lambda c0: pl.BlockSpec((2 * tq, DA_VALUE_DIM),
                                    lambda b, h, s, qt, kt, kd: (b * nkb + kt[s], c0 * heads + h))
    vec = lambda n: pl.BlockSpec((1, n), lambda b, h, s, qt, kt, kd: (0, 0))
    kern = functools.partial(_flash_kernel, lambda_init=lambda_init)
    grid_spec = pltpu.PrefetchScalarGridSpec(
        num_scalar_prefetch=3,
        grid=(batch, heads, len(steps)),
        in_specs=[
            q_blk(0), k_blk(1), k_blk(2), q_blk(3),
            vec(DA_HEAD_DIM), vec(DA_HEAD_DIM), vec(DA_HEAD_DIM), vec(DA_HEAD_DIM), vec(DA_VALUE_DIM),
        ],
        out_specs=q_blk(0),
        scratch_shapes=[pltpu.VMEM((2, tq, LANES), F32), pltpu.VMEM((2, tq, LANES), F32),
                        pltpu.VMEM((2, tq, DA_VALUE_DIM), F32)],
    )
    return pl.pallas_call(
        kern,
        grid_spec=grid_spec,
        out_shape=jax.ShapeDtypeStruct((m, heads * DA_VALUE_DIM), BF16),
        compiler_params=_params(("parallel", "parallel", "arbitrary")),
        name="flash_diff_attention",
    )(qtab, ktab, kinds, proj, proj, proj, proj, lam_q1.reshape(1, -1), lam_k1.reshape(1, -1),
      lam_q2.reshape(1, -1), lam_k2.reshape(1, -1), subln_w.reshape(1, -1))


def _rwkv_prep_kernel(r_ref, k_ref, v_ref, lora_ref, vfirst_ref, w0_ref, a0_ref, kk_ref, ka_ref, rk_ref, v0_ref,
                      dup_ref, iup_ref, vup_ref,
                      at_ref, bt_ref, kt_ref, rt_ref, vo_ref, bonus_ref, gend_ref, *, vres):
    ones_blk = _half_ones()
    ch = WKV_CHUNK
    lora = lora_ref[...]
    dwt = jnp.tanh(lora[:, 0:LORA_PAD]).astype(BF16)
    da = lora[:, LORA_PAD:2 * LORA_PAD].astype(BF16)
    z = -(w0_ref[...] + _dot(dwt, dup_ref[...]))
    w_log = -(jnp.maximum(z, 0.0) + jnp.log1p(jnp.exp(-jnp.abs(z)))) - 0.5
    lw = -jnp.exp(w_log)
    a = _sigmoid(a0_ref[...] + _dot(da, iup_ref[...]))
    r = r_ref[...]
    k = k_ref[...]
    v = v_ref[...]
    if vres:
        pv = lora[:, 2 * LORA_PAD:3 * LORA_PAD].astype(BF16)
        v = v + (vfirst_ref[...] - v) * _sigmoid(v0_ref[...] + _dot(pv, vup_ref[...]))
    vo_ref[...] = v.astype(BF16)
    kk = k * kk_ref[...]
    k_mod = k * (1.0 + (a - 1.0) * ka_ref[...])
    rkr = r * k_mod * rk_ref[...]

    ri = lax.broadcasted_iota(jnp.int32, (ch, ch), 0)
    ci = lax.broadcasted_iota(jnp.int32, (ch, ch), 1)
    tri = jnp.where(ci <= ri, 1.0, 0.0).astype(BF16)
    lw_hi, lw_lo = _split2(lw)
    for t in range(r.shape[0] // ch):
        rows = slice(t * ch, (t + 1) * ch)
        cum = _dot(tri, lw_hi[rows]) + _dot(tri, lw_lo[rows])
        g_in = jnp.exp(cum)
        g_inv = jnp.exp(-cum)
        g_ex = jnp.exp(cum - lw[rows])
        gend_ref[t] = g_in[ch - 1:ch, :]
        rt_ref[rows, :] = (r[rows] * g_in).astype(BF16)
        kt_ref[rows, :] = (k_mod[rows] * g_inv).astype(BF16)
        for c in range(r.shape[1] // LANES):
            sl = slice(c * LANES, (c + 1) * LANES)
            kc = kk[rows, sl]
            nrm = jnp.maximum(jnp.sqrt(_segsum(kc * kc, ones_blk)), 1e-12)
            kn = kc / nrm
            at_ref[rows, sl] = (-kn * g_ex[:, sl]).astype(BF16)
            bt_ref[rows, sl] = (kn * a[rows, sl] * g_inv[:, sl]).astype(BF16)
            bonus_ref[rows, sl] = _segsum(rkr[rows, sl], ones_blk) * v[rows, sl]


def _rwkv_prep(pm, lora, pm_first, w0, a0, k_k, k_a, r_k, v0, dup, iup, vup, *, width, vres, tr, tc):
    m = pm.shape[0]
    ncb = width // tc
    cpt = tr // WKV_CHUNK
    col = lambda c0: pl.BlockSpec((tr, tc), lambda i, j: (i, c0 * ncb + j))
    vec = pl.BlockSpec((1, tc), lambda i, j: (0, j))
    up = pl.BlockSpec((LORA_PAD, tc), lambda i, j: (0, j))
    kern = functools.partial(_rwkv_prep_kernel, vres=vres)
    row = lambda a: a.reshape(1, width)
    return pl.pallas_call(
        kern,
        grid=(m // tr, ncb),
        in_specs=[col(0), col(1), col(2), pl.BlockSpec((tr, lora.shape[1]), lambda i, j: (i, 0)), col(2),
                  vec, vec, vec, vec, vec, vec, up, up, up],
        out_specs=[pl.BlockSpec((tr, tc), lambda i, j: (i, j))] * 6
                  + [pl.BlockSpec((cpt, 1, tc), lambda i, j: (i, 0, j))],
        out_shape=[jax.ShapeDtypeStruct((m, width), BF16)] * 5 + [jax.ShapeDtypeStruct((m, width), F32)]
                  + [jax.ShapeDtypeStruct((m // WKV_CHUNK, 1, width), F32)],
        compiler_params=_params(("parallel", "parallel")),
        name="rwkv_prep",
    )(pm, pm, pm, lora, pm_first, row(w0), row(a0), row(k_k), row(k_a), row(r_k), row(v0), dup, iup, vup)


def _unit_lower_inverse(lows, row, col):
    blk = lambda sh: (row >> sh) == (col >> sh)
    eye = jnp.where(row == col, 1.0, 0.0)
    d1 = [jnp.where(blk(3), lo, 0.0) for lo in lows]
    d2 = [_mm(d, d) for d in d1]
    d4 = [_mm(d, d) for d in d2]
    xs = [eye + a + b + _mm(a, b) for a, b in zip(d1, d2)]
    xs = [x + _mm(x, d) for x, d in zip(xs, d4)]
    for sh in (3, 4, 5):
        off = jnp.logical_and(blk(sh + 1), jnp.logical_not(blk(sh)))
        ns = [jnp.where(off, lo, 0.0) for lo in lows]
        ts = [_mm(x, n) for x, n in zip(xs, ns)]
        xs = [x + _mm(t, x) for x, t in zip(xs, ts)]
    return xs


def _wkv_kernel(at_ref, bt_ref, kt_ref, rt_ref, v_ref, gend_ref, bonus_ref, g_ref, gnw_ref, gnb_ref, o_ref, s_ref):
    c = pl.program_id(2)
    units = range(at_ref.shape[1] // LANES)
    ch = WKV_CHUNK
    n2 = 2 * ch

    @pl.when(c == 0)
    def _():
        s_ref[...] = jnp.zeros(s_ref.shape, F32)

    row = lax.broadcasted_iota(jnp.int32, (n2, n2), 0)
    col = lax.broadcasted_iota(jnp.int32, (n2, n2), 1)
    strict = col < row
    incl = col <= row
    lane = lax.broadcasted_iota(jnp.int32, (ch, LANES), 1)
    m0 = jnp.where(lane < RW_HEAD_DIM, 1.0, 0.0).astype(BF16)
    m1 = jnp.where(lane < RW_HEAD_DIM, 0.0, 1.0).astype(BF16)
    ones_blk = _half_ones()
    sls = [slice(u * LANES, (u + 1) * LANES) for u in units]

    def stack2(ref):
        out = []
        for sl in sls:
            x = ref[:, sl]
            out.append(jnp.concatenate([x * m0, x * m1], axis=0))
        return out

    a2, b2, k2, r2, v2 = stack2(at_ref), stack2(bt_ref), stack2(kt_ref), stack2(rt_ref), stack2(v_ref)
    bk = [jnp.concatenate([b, k], axis=0) for b, k in zip(b2, k2)]
    sc = [_mm(jnp.concatenate([a, r], axis=0), x, nt=True) for a, r, x in zip(a2, r2, bk)]
    low_ab = [jnp.where(strict, s[:n2, :n2], 0.0) for s in sc]
    low_ak = [jnp.where(strict, s[:n2, n2:], 0.0) for s in sc]
    low_rb = [jnp.where(incl, s[n2:, :n2], 0.0) for s in sc]
    low_rk = [jnp.where(incl, s[n2:, n2:], 0.0) for s in sc]

    w1 = [_mm(l, v) for l, v in zip(low_ak, v2)]
    rkv = [_mm(l, v) for l, v in zip(low_rk, v2)]
    tinv = _unit_lower_inverse(low_ab, row, col)
    au = [_mm(t, jnp.concatenate([a, w.astype(BF16)], axis=1)) for t, a, w in zip(tinv, a2, w1)]
    ry = [_mm(l, x) for l, x in zip(low_rb, au)]

    ss = [s_ref[u] for u in units]
    y2 = [_mm(r + x[:, :n2], s, nt=True) + (x[:, n2:] + z) for r, x, s, z in zip(r2, ry, ss, rkv)]
    ps = [_mm(x[:, :n2].T, b) for x, b in zip(au, b2)]
    qs = [_mm(jnp.concatenate([x[:, n2:], v.astype(F32)], axis=0).T, y) for x, v, y in zip(au, v2, bk)]
    sp = [_mm(s, p) for s, p in zip(ss, ps)]
    for u in units:
        s_ref[u] = (ss[u] + sp[u] + qs[u]) * gend_ref[0, :, sls[u]]

    inv_n = 1.0 / RW_HEAD_DIM
    ys = [y[:ch] + y[ch:] for y in y2]
    ds = [y - _segsum(y, ones_blk) * inv_n for y in ys]
    var = [_segsum(d * d, ones_blk) * inv_n for d in ds]
    for u in units:
        sl = sls[u]
        yn = ds[u] * lax.rsqrt(var[u] + GN_EPS) * gnw_ref[:, sl] + gnb_ref[:, sl]
        g = g_ref[:, sl]
        o_ref[:, sl] = ((yn + bonus_ref[:, sl]) * (g * _sigmoid(g))).astype(BF16)


def _wkv(pm, at, bt, kt, rt, v, gend, bonus, gn_w, gn_b, *, batch, seq, width, units):
    m = pm.shape[0]
    nc = seq // WKV_CHUNK
    tc = units * LANES
    ncb = width // tc
    blk = pl.BlockSpec((WKV_CHUNK, tc), lambda b, j, c: (b * nc + c, j))
    gate = pl.BlockSpec((WKV_CHUNK, tc), lambda b, j, c: (b * nc + c, 3 * ncb + j))
    vec = pl.BlockSpec((1, tc), lambda b, j, c: (0, j))
    return pl.pallas_call(
        _wkv_kernel,
        grid=(batch, ncb, nc),
        in_specs=[blk, blk, blk, blk, blk, pl.BlockSpec((1, 1, tc), lambda b, j, c: (b * nc + c, 0, j)),
                  blk, gate, vec, vec],
        out_specs=blk,
        out_shape=jax.ShapeDtypeStruct((m, width), BF16),
        scratch_shapes=[pltpu.VMEM((units, 2 * WKV_CHUNK, 2 * WKV_CHUNK), F32)],
        compiler_params=_params(("parallel", "parallel", "arbitrary")),
        name="wkv7_chunked",
    )(at, bt, kt, rt, v, gend, bonus, pm, gn_w.reshape(1, width), gn_b.reshape(1, width))


def _tiles(m, seq):
    return dict(tm=min(1024, seq), tn=min(1024, 4 * DA_VALUE_DIM), tr=min(512, seq), tq=min(512, seq))


def _attention_layer(x, cos, sin, norm_w, w_in, q_gain, k_gain, lq1, lk1, lq2, lk2, subln_w, w_out, lambda_init,
                     *, batch, seq, t):
    d = x.shape[1]
    heads = d // DA_VALUE_DIM
    proj = _attn_in_proj(x, norm_w, w_in.astype(BF16), cos, sin, q_gain, k_gain, tm=t["tm"], tn=min(t["tn"], d))
    o = _flash_diff_attention(proj, lq1, lk1, lq2, lk2, subln_w, batch=batch, seq=seq, heads=heads,
                              lambda_init=lambda_init, tq=t["tq"])
    return _matmul_residual(o, w_out.astype(BF16), x, tm=t["tm"], tn=min(t["tn"], d))


def _pad_rows(w, rows):
    return jnp.pad(w, ((0, rows - w.shape[0]), (0, 0)))


def _pad_cols(w, cols):
    return jnp.pad(w, ((0, 0), (0, cols - w.shape[1])))


def _rwkv_layer(x, pm_first, norm_w, w_in, mu, w0, decay_up, a0, iclr_up, k_k, k_a, r_k, gn_w, gn_b, w_out, vres,
                *, batch, seq, t):
    d = x.shape[1]
    wide = 4 * d
    dr = decay_up.shape[0]
    ir = iclr_up.shape[0]
    lora_w = [_pad_cols(w_in[:, wide:wide + dr], LORA_PAD), _pad_cols(w_in[:, wide + dr:wide + dr + ir], LORA_PAD)]
    lora_mu = [jnp.pad(mu[wide:wide + dr], (0, LORA_PAD - dr)), jnp.pad(mu[wide + dr:wide + dr + ir], (0, LORA_PAD - ir))]
    if vres is not None:
        vd_w, vd_mu, v0, vu_w = vres
        lora_w.append(_pad_cols(vd_w, LORA_PAD))
        lora_mu.append(jnp.pad(vd_mu, (0, LORA_PAD - vd_mu.shape[0])))
        vup = _pad_rows(vu_w, LORA_PAD).astype(BF16)
    else:
        v0 = jnp.zeros((d,), F32)
        vup = jnp.zeros((LORA_PAD, d), BF16)
    lora_w = jnp.concatenate(lora_w, axis=1).astype(BF16)
    lora_mu = jnp.concatenate(lora_mu)
    pm = _rwkv_in_proj(x, norm_w, w_in[:, :wide].astype(BF16), mu[:wide], rows_per_seq=seq, tm=t["tm"], tn=t["tn"])
    lora = _rwkv_in_proj(x, norm_w, lora_w, lora_mu, rows_per_seq=seq, tm=t["tm"], tn=lora_w.shape[1])
    at, bt, kt, rt, v, bonus, gend = _rwkv_prep(
        pm, lora, pm if pm_first is None else pm_first, w0, a0, k_k, k_a, r_k.reshape(-1), v0,
        _pad_rows(decay_up, LORA_PAD).astype(BF16), _pad_rows(iclr_up, LORA_PAD).astype(BF16), vup,
        width=d, vres=vres is not None, tr=t["tr"], tc=min(512, d))
    o = _wkv(pm, at, bt, kt, rt, v, gend, bonus, gn_w, gn_b, batch=batch, seq=seq, width=d,
             units=min(WKV_UNITS, d // LANES))
    return _matmul_residual(o, w_out.astype(BF16), x, tm=t["tm"], tn=min(t["tn"], d)), pm


def kernel(x, positions, norm_w, da_w_in, da_q_gain, da_k_gain, da_lam_q1, da_lam_k1, da_lam_q2, da_lam_k2, da_subln_w, da_w_out, rw_w_in, rw_mu, rw_w0, rw_decay_up, rw_a0, rw_iclr_up, rw_k_k, rw_k_a, rw_r_k, rw_gn_w, rw_gn_b, rw_w_out, rw_vres_down, rw_vres_mu, rw_v0, rw_vres_up):
    batch, seq, d = x.shape
    depth = norm_w.shape[0]
    m = batch * seq
    t = _tiles(m, seq)
    xf = x.reshape(m, d)
    cos, sin = _rope_tables(positions.reshape(m), tr=t["tr"])
    pm_first = None
    for i in range(depth):
        j = i // 2
        if i % 2 == 0:
            lambda_init = 0.8 - 0.6 * math.exp(-0.3 * i)
            xf = _attention_layer(xf, cos, sin, norm_w[i], da_w_in[j], da_q_gain[j], da_k_gain[j], da_lam_q1[j],
                                  da_lam_k1[j], da_lam_q2[j], da_lam_k2[j], da_subln_w[j], da_w_out[j], lambda_init,
                                  batch=batch, seq=seq, t=t)
        else:
            vres = None if j == 0 else (rw_vres_down[j - 1], rw_vres_mu[j - 1], rw_v0[j - 1], rw_vres_up[j - 1])
            xf, pm_now = _rwkv_layer(xf, pm_first, norm_w[i], rw_w_in[j], rw_mu[j], rw_w0[j], rw_decay_up[j], rw_a0[j],
                                     rw_iclr_up[j], rw_k_k[j], rw_k_a[j], rw_r_k[j], rw_gn_w[j], rw_gn_b[j],
                                     rw_w_out[j], vres, batch=batch, seq=seq, t=t)
            if pm_first is None:
                pm_first = pm_now
    return xf.reshape(batch, seq, d)
```

```python
import functools
import math

import jax
import jax.numpy as jnp
from jax import lax
from jax.experimental import pallas as pl
from jax.experimental.pallas import tpu as pltpu

F32 = jnp.float32
BF16 = jnp.bfloat16

LANES = 128
DA_HEAD_DIM = 128
DA_VALUE_DIM = 2 * DA_HEAD_DIM
RW_HEAD_DIM = 64
WKV_CHUNK = 64
FLASH_GRANULES = 4
WKV_UNITS = 16
LORA_PAD = 128
ROPE_THETA = 10000.0
NORM_EPS = 1e-6
GN_EPS = 64e-5
NEG_BIG = -1e30
VMEM_LIMIT = 56 * 1024 * 1024


def _params(sem):
    return pltpu.CompilerParams(dimension_semantics=sem, vmem_limit_bytes=VMEM_LIMIT)


def _dot(a, b):
    return jnp.dot(a, b, preferred_element_type=F32)


def _dot_nt(a, b):
    return lax.dot_general(a, b, (((1,), (1,)), ((), ())), preferred_element_type=F32)


def _split2(x):
    hi = x.astype(BF16)
    lo = (x - hi.astype(F32)).astype(BF16)
    return hi, lo


def _mm(a, b, nt=False):
    d = _dot_nt if nt else _dot
    return d(a.astype(BF16), b.astype(BF16))


def _sigmoid(x):
    return 1.0 / (1.0 + jnp.exp(-x))


def _store_normed(x_ref, g_ref, xn_ref):
    x = x_ref[...]
    ms = jnp.mean(x * x, axis=-1, keepdims=True)
    xn_ref[...] = (x * lax.rsqrt(ms + NORM_EPS) * g_ref[...]).astype(BF16)


def _rwkv_in_kernel(x_ref, g_ref, w_ref, mu_ref, o_ref, xn_ref, prev_ref, *, rows_per_seq):
    i = pl.program_id(0)
    j = pl.program_id(1)
    tm = x_ref.shape[0]

    @pl.when(j == 0)
    def _():
        _store_normed(x_ref, g_ref, xn_ref)

    @pl.when(i == 0)
    def _():
        prev_ref[j] = jnp.zeros(prev_ref.shape[1:], F32)

    acc = _dot(xn_ref[...], w_ref[...])
    seq_start = (i * tm) % rows_per_seq == 0
    prev = jnp.where(seq_start, 0.0, prev_ref[j])
    row = lax.broadcasted_iota(jnp.int32, acc.shape, 0)
    shifted = jnp.where(row == 0, prev, pltpu.roll(acc, 1, axis=0))
    prev_ref[j] = acc[tm - 1:tm, :]
    o_ref[...] = acc + (shifted - acc) * mu_ref[...]


def _rwkv_in_proj(x, gain, w_bf16, mu, *, rows_per_seq, tm, tn):
    m, d = x.shape
    n = mu.shape[0]
    assert m % tm == 0 and n % tn == 0 and rows_per_seq % tm == 0 and n <= w_bf16.shape[1]
    return pl.pallas_call(
        functools.partial(_rwkv_in_kernel, rows_per_seq=rows_per_seq),
        grid=(m // tm, n // tn),
        in_specs=[
            pl.BlockSpec((tm, d), lambda i, j: (i, 0)),
            pl.BlockSpec((1, d), lambda i, j: (0, 0)),
            pl.BlockSpec((d, tn), lambda i, j: (0, j)),
            pl.BlockSpec((1, tn), lambda i, j: (0, j)),
        ],
        out_specs=pl.BlockSpec((tm, tn), lambda i, j: (i, j)),
        out_shape=jax.ShapeDtypeStruct((m, n), F32),
        scratch_shapes=[pltpu.VMEM((tm, d), BF16), pltpu.VMEM((n // tn, 1, tn), F32)],
        compiler_params=_params(("arbitrary", "arbitrary")),
        name="rwkv_in_proj",
    )(x, gain.reshape(1, d), w_bf16, mu.reshape(1, n))


def _attn_in_kernel(x_ref, g_ref, w_ref, cos_ref, sin_ref, qg_ref, kg_ref, o_ref, xn_ref, acc_ref, *, sec, scale):
    j = pl.program_id(1)
    last = pl.num_programs(1) - 1

    def matmul():
        acc_ref[...] = _dot(xn_ref[...], w_ref[...])

    def finish_qk():
        is_q = j - 1 < sec
        gain = jnp.where(is_q, qg_ref[...], kg_ref[...])
        mult = jnp.where(is_q, scale, 1.0)
        cos = cos_ref[...]
        lane = lax.broadcasted_iota(jnp.int32, cos.shape, 1)
        sin_signed = jnp.where(lane < DA_HEAD_DIM // 2, -sin_ref[...], sin_ref[...])
        for c in range(acc_ref.shape[1] // DA_HEAD_DIM):
            sl = slice(c * DA_HEAD_DIM, (c + 1) * DA_HEAD_DIM)
            x = acc_ref[:, sl]
            y = x * lax.rsqrt(jnp.mean(x * x, axis=-1, keepdims=True) + NORM_EPS) * gain
            rot = pltpu.roll(y, DA_HEAD_DIM // 2, axis=1)
            o_ref[:, sl] = ((y * cos + rot * sin_signed) * mult).astype(BF16)

    def finish_plain():
        o_ref[...] = acc_ref[...].astype(BF16)

    @pl.when(j == 0)
    def _():
        _store_normed(x_ref, g_ref, xn_ref)
        matmul()

    @pl.when(jnp.logical_and(j >= 1, j <= 2 * sec))
    def _():
        finish_qk()
        matmul()

    @pl.when(jnp.logical_and(j > 2 * sec, j < last))
    def _():
        finish_plain()
        matmul()

    @pl.when(j == last)
    def _():
        finish_plain()


def _attn_in_proj(x, gain, w_bf16, cos, sin, q_gain, k_gain, *, tm, tn):
    m, d = x.shape
    n = w_bf16.shape[1]
    nj = n // tn
    assert m % tm == 0 and d % tn == 0 and n == 4 * d
    rope = pl.BlockSpec((tm, DA_HEAD_DIM), lambda i, j: (i, 0))
    vec = pl.BlockSpec((1, DA_HEAD_DIM), lambda i, j: (0, 0))
    return pl.pallas_call(
        functools.partial(_attn_in_kernel, sec=d // tn, scale=DA_HEAD_DIM ** -0.5 * math.log2(math.e)),
        grid=(m // tm, nj + 1),
        in_specs=[
            pl.BlockSpec((tm, d), lambda i, j: (i, 0)),
            pl.BlockSpec((1, d), lambda i, j: (0, 0)),
            pl.BlockSpec((d, tn), lambda i, j: (0, jnp.minimum(j, nj - 1))),
            rope, rope, vec, vec,
        ],
        out_specs=pl.BlockSpec((tm, tn), lambda i, j: (i, jnp.maximum(j - 1, 0))),
        out_shape=jax.ShapeDtypeStruct((m, n), BF16),
        scratch_shapes=[pltpu.VMEM((tm, d), BF16), pltpu.VMEM((tm, tn), F32)],
        compiler_params=_params(("parallel", "arbitrary")),
        name="attn_in_proj",
    )(x, gain.reshape(1, d), w_bf16, cos, sin, q_gain.reshape(1, -1), k_gain.reshape(1, -1))


def _mm_res_kernel(a_ref, w_ref, x_ref, o_ref):
    o_ref[...] = x_ref[...] + _dot(a_ref[...], w_ref[...])


def _matmul_residual(a_bf16, w_bf16, x, *, tm, tn):
    m, k = a_bf16.shape
    n = w_bf16.shape[1]
    assert m % tm == 0 and n % tn == 0
    return pl.pallas_call(
        _mm_res_kernel,
        grid=(m // tm, n // tn),
        in_specs=[
            pl.BlockSpec((tm, k), lambda i, j: (i, 0)),
            pl.BlockSpec((k, tn), lambda i, j: (0, j)),
            pl.BlockSpec((tm, tn), lambda i, j: (i, j)),
        ],
        out_specs=pl.BlockSpec((tm, tn), lambda i, j: (i, j)),
        out_shape=jax.ShapeDtypeStruct((m, n), F32),
        compiler_params=_params(("parallel", "arbitrary")),
        name="out_proj_residual",
    )(a_bf16, w_bf16, x)


def _rope_kernel(pos_ref, inv_ref, cos_ref, sin_ref):
    ang = pos_ref[...].astype(F32) * inv_ref[...]
    cos_ref[...] = jnp.cos(ang)
    sin_ref[...] = jnp.sin(ang)


def _rope_tables(positions_flat, *, tr):
    m = positions_flat.shape[0]
    inv = ROPE_THETA ** (-jnp.arange(0, DA_HEAD_DIM, 2, dtype=F32) / DA_HEAD_DIM)
    inv = jnp.concatenate([inv, inv]).reshape(1, DA_HEAD_DIM)
    return pl.pallas_call(
        _rope_kernel,
        grid=(m // tr,),
        in_specs=[pl.BlockSpec((tr, 1), lambda i: (i, 0)), pl.BlockSpec((1, DA_HEAD_DIM), lambda i: (0, 0))],
        out_specs=[pl.BlockSpec((tr, DA_HEAD_DIM), lambda i: (i, 0))] * 2,
        out_shape=[jax.ShapeDtypeStruct((m, DA_HEAD_DIM), F32)] * 2,
        compiler_params=_params(("parallel",)),
        name="rope_tables",
    )(positions_flat.reshape(m, 1), inv)


def _flash_kernel(qtab_ref, ktab_ref, kind_ref, q_ref, k_ref, v_ref, gate_ref, lq1_ref, lk1_ref, lq2_ref, lk2_ref,
                  sub_ref, o_ref, m_ref, l_ref, acc_ref, *, lambda_init):
    step_id = pl.program_id(2)
    kb = ktab_ref[step_id]
    kind = kind_ref[step_id]
    tq = q_ref.shape[0]
    per_block = k_ref.shape[0] // tq
    comps = range(2)
    sls = [slice(c * DA_HEAD_DIM, (c + 1) * DA_HEAD_DIM) for c in comps]

    @pl.when(kb == 0)
    def _():
        m_ref[...] = jnp.full(m_ref.shape, NEG_BIG, F32)
        l_ref[...] = jnp.zeros(l_ref.shape, F32)
        acc_ref[...] = jnp.zeros(acc_ref.shape, F32)

    def process(granules, diag):
        keys = granules * tq
        reps = keys // LANES
        s = [_dot_nt(q_ref[:, sl], k_ref[0:keys, sl]) for sl in sls]
        if diag:
            row = lax.broadcasted_iota(jnp.int32, (tq, tq), 0)
            col = lax.broadcasted_iota(jnp.int32, (tq, tq), 1)
            keep = col <= row
            s = [jnp.concatenate([x[:, :keys - tq], jnp.where(keep, x[:, keys - tq:], NEG_BIG)], axis=1)
                 if keys > tq else jnp.where(keep, x, NEG_BIG) for x in s]
        v = v_ref[0:keys, :]
        m, l, acc = [], [], []
        for c in comps:
            m_new = jnp.maximum(m_ref[c], jnp.max(s[c], axis=-1, keepdims=True))
            alpha = jnp.exp2(m_ref[c] - m_new)
            p = jnp.exp2(s[c] - jnp.concatenate([m_new] * reps, axis=1))
            part = p[:, 0:LANES]
            for j in range(1, reps):
                part = part + p[:, j * LANES:(j + 1) * LANES]
            m.append(m_new)
            l.append(alpha * l_ref[c] + part)
            acc.append(jnp.concatenate([alpha] * (DA_VALUE_DIM // LANES), axis=1) * acc_ref[c] + _dot(p.astype(BF16), v))
        return m, l, acc

    def carry(m, l, acc):
        for c in comps:
            m_ref[c] = m[c]
            l_ref[c] = l[c]
            acc_ref[c] = acc[c]

    def finish(l, acc):
        lam = (jnp.exp(jnp.sum(lq1_ref[...] * lk1_ref[...], axis=-1, keepdims=True))
               - jnp.exp(jnp.sum(lq2_ref[...] * lk2_ref[...], axis=-1, keepdims=True)) + lambda_init)
        l0 = jnp.sum(l[0], axis=-1, keepdims=True)
        l1 = jnp.sum(l[1], axis=-1, keepdims=True)
        o = acc[0] / l0 - lam * (acc[1] / l1)
        o = o * lax.rsqrt(jnp.mean(o * o, axis=-1, keepdims=True) + NORM_EPS) * sub_ref[...]
        g = gate_ref[...].astype(F32)
        o_ref[...] = (o * (1.0 - lambda_init) * (g * _sigmoid(g))).astype(BF16)

    @pl.when(kind == 0)
    def _():
        carry(*process(per_block, False))

    for n in range(1, per_block + 1):
        @pl.when(kind == n)
        def _(n=n):
            _, l, acc = process(n, True)
            finish(l, acc)


def _flash_diff_attention(proj, lam_q1, lam_k1, lam_q2, lam_k2, subln_w, *, batch, seq, heads,
                          lambda_init, tq):
    m = proj.shape[0]
    nq = seq // tq
    per = min(FLASH_GRANULES, nq)
    nkb = nq // per
    assert nq % per == 0
    steps = [(a, b, 0 if b < a // per else 1 + a % per) for a in range(nq) for b in range(a // per + 1)]
    qtab, ktab, kinds = (jnp.asarray(col, jnp.int32) for col in zip(*steps))
    q_blk = lambda c0: pl.BlockSpec((tq, DA_VALUE_DIM),
                                    lambda b, h, s, qt, kt, kd: (b * nq + qt[s], c0 * heads + h))
    k_blk = lambda c0: pl.BlockSpec((per * tq, DA_VALUE_DIM),
                                    lambda b, h, s, qt, kt, kd: (b * nkb + kt[s], c0 * heads + h))
    vec = lambda n: pl.BlockSpec((1, n), lambda b, h, s, qt, kt, kd: (0, 0))
    kern = functools.partial(_flash_kernel, lambda_init=lambda_init)
    grid_spec = pltpu.PrefetchScalarGridSpec(
        num_scalar_prefetch=3,
        grid=(batch, heads, len(steps)),
        in_specs=[
            q_blk(0), k_blk(1), k_blk(2), q_blk(3),
            vec(DA_HEAD_DIM), vec(DA_HEAD_DIM), vec(DA_HEAD_DIM), vec(DA_HEAD_DIM), vec(DA_VALUE_DIM),
        ],
        out_specs=q_blk(0),
        scratch_shapes=[pltpu.VMEM((2, tq, LANES), F32), pltpu.VMEM((2, tq, LANES), F32),
                        pltpu.VMEM((2, tq, DA_VALUE_DIM), F32)],
    )
    return pl.pallas_call(
        kern,
        grid_spec=grid_spec,
        out_shape=jax.ShapeDtypeStruct((m, heads * DA_VALUE_DIM), BF16),
        compiler_params=_params(("parallel", "parallel", "arbitrary")),
        name="flash_diff_attention",
    )(qtab, ktab, kinds, proj, proj, proj, proj, lam_q1.reshape(1, -1), lam_k1.reshape(1, -1),
      lam_q2.reshape(1, -1), lam_k2.reshape(1, -1), subln_w.reshape(1, -1))


def _unit_lower_inverse(lows, row, col):
    blk = lambda sh: (row >> sh) == (col >> sh)
    eye = jnp.where(row == col, 1.0, 0.0)
    d1 = [jnp.where(blk(3), lo, 0.0) for lo in lows]
    d2 = [_mm(d, d) for d in d1]
    d4 = [_mm(d, d) for d in d2]
    xs = [eye + a + b + _mm(a, b) for a, b in zip(d1, d2)]
    xs = [x + _mm(x, d) for x, d in zip(xs, d4)]
    for sh in (3, 4, 5):
        off = jnp.logical_and(blk(sh + 1), jnp.logical_not(blk(sh)))
        ns = [jnp.where(off, lo, 0.0) for lo in lows]
        ts = [_mm(x, n) for x, n in zip(xs, ns)]
        xs = [x + _mm(t, x) for x, t in zip(xs, ts)]
    return xs


def _wkv_kernel(r_ref, k_ref, v_ref, g_ref, lora_ref, vfirst_ref, w0_ref, a0_ref, kk_ref, ka_ref, rk_ref, v0_ref,
                gnw_ref, gnb_ref, dup_ref, iup_ref, vup_ref, o_ref, s_ref, *, vres):
    c = pl.program_id(2)
    units = range(r_ref.shape[1] // LANES)
    ch = WKV_CHUNK
    n2 = 2 * ch

    @pl.when(c == 0)
    def _():
        s_ref[...] = jnp.zeros(s_ref.shape, F32)

    lora = lora_ref[...]
    dwt = jnp.tanh(lora[:, 0:LORA_PAD]).astype(BF16)
    da = lora[:, LORA_PAD:2 * LORA_PAD].astype(BF16)
    z = -(w0_ref[...] + _dot(dwt, dup_ref[...]))
    w_log = -(jnp.maximum(z, 0.0) + jnp.log1p(jnp.exp(-jnp.abs(z)))) - 0.5
    lw = -jnp.exp(w_log)
    a = _sigmoid(a0_ref[...] + _dot(da, iup_ref[...]))
    r = r_ref[...]
    k = k_ref[...]
    v = v_ref[...]
    if vres:
        pv = lora[:, 2 * LORA_PAD:3 * LORA_PAD].astype(BF16)
        v = v + (vfirst_ref[...] - v) * _sigmoid(v0_ref[...] + _dot(pv, vup_ref[...]))
    kk = k * kk_ref[...]
    k_mod = k * (1.0 + (a - 1.0) * ka_ref[...])
    rkr = r * k_mod * rk_ref[...]
    r64 = lax.broadcasted_iota(jnp.int32, (ch, ch), 0)
    c64 = lax.broadcasted_iota(jnp.int32, (ch, ch), 1)
    tri = jnp.where(c64 <= r64, 1.0, 0.0).astype(BF16)
    lw_hi, lw_lo = _split2(lw)
    cum = _dot(tri, lw_hi) + _dot(tri, lw_lo)
    g_in = jnp.exp(cum)
    g_inv = jnp.exp(-cum)
    g_ex = jnp.exp(cum - lw)
    rt = (r * g_in).astype(BF16)
    kt = (k_mod * g_inv).astype(BF16)
    vb = v.astype(BF16)

    row = lax.broadcasted_iota(jnp.int32, (n2, n2), 0)
    col = lax.broadcasted_iota(jnp.int32, (n2, n2), 1)
    strict = col < row
    incl = col <= row
    lane = lax.broadcasted_iota(jnp.int32, (ch, LANES), 1)
    first = lane < RW_HEAD_DIM
    m0 = jnp.where(first, 1.0, 0.0)
    m1 = 1.0 - m0
    sls = [slice(u * LANES, (u + 1) * LANES) for u in units]

    def segsum(x):
        s0 = jnp.sum(x * m0, axis=-1, keepdims=True)
        s1 = jnp.sum(x * m1, axis=-1, keepdims=True)
        return jnp.where(first, s0, s1)

    def stack2(x):
        return jnp.concatenate([jnp.where(first, x, 0), jnp.where(first, 0, x)], axis=0)

    kn = []
    for sl in sls:
        kc = kk[:, sl]
        kn.append(kc * lax.rsqrt(jnp.maximum(segsum(kc * kc), 1e-24)))
    a2 = [stack2((-x * g_ex[:, sl]).astype(BF16)) for x, sl in zip(kn, sls)]
    b2 = [stack2((x * a[:, sl] * g_inv[:, sl]).astype(BF16)) for x, sl in zip(kn, sls)]
    k2 = [stack2(kt[:, sl]) for sl in sls]
    r2 = [stack2(rt[:, sl]) for sl in sls]
    v2 = [stack2(vb[:, sl]) for sl in sls]

    bk = [jnp.concatenate([b, k], axis=0) for b, k in zip(b2, k2)]
    sc = [_mm(jnp.concatenate([a, r], axis=0), x, nt=True) for a, r, x in zip(a2, r2, bk)]
    low_ab = [jnp.where(strict, s[:n2, :n2], 0.0) for s in sc]
    low_ak = [jnp.where(strict, s[:n2, n2:], 0.0) for s in sc]
    low_rb = [jnp.where(incl, s[n2:, :n2], 0.0) for s in sc]
    low_rk = [jnp.where(incl, s[n2:, n2:], 0.0) for s in sc]

    w1 = [_mm(l, v) for l, v in zip(low_ak, v2)]
    rkv = [_mm(l, v) for l, v in zip(low_rk, v2)]
    tinv = _unit_lower_inverse(low_ab, row, col)
    au = [_mm(t, jnp.concatenate([a, w.astype(BF16)], axis=1)) for t, a, w in zip(tinv, a2, w1)]
    ry = [_mm(l, x) for l, x in zip(low_rb, au)]

    ss = [s_ref[u] for u in units]
    y2 = [_mm(r + x[:, :n2], s, nt=True) + (x[:, n2:] + z) for r, x, s, z in zip(r2, ry, ss, rkv)]
    ps = [_mm(x[:, :n2].T, b) for x, b in zip(au, b2)]
    qs = [_mm(jnp.concatenate([x[:, n2:], v.astype(F32)], axis=0).T, y) for x, v, y in zip(au, v2, bk)]
    sp = [_mm(s, p) for s, p in zip(ss, ps)]
    for u in units:
        s_ref[u] = (ss[u] + sp[u] + qs[u]) * g_in[ch - 1:ch, sls[u]]

    inv_n = 1.0 / RW_HEAD_DIM
    ys = [y[:ch] + y[ch:] for y in y2]
    ds = [y - segsum(y) * inv_n for y in ys]
    var = [segsum(d * d) * inv_n for d in ds]
    for u in units:
        sl = sls[u]
        yn = ds[u] * lax.rsqrt(var[u] + GN_EPS) * gnw_ref[:, sl] + gnb_ref[:, sl]
        bonus = segsum(rkr[:, sl]) * v[:, sl]
        g = g_ref[:, sl]
        o_ref[:, sl] = ((yn + bonus) * (g * _sigmoid(g))).astype(BF16)


def _wkv(pm, lora, pm_first, w0, a0, k_k, k_a, r_k, v0, gn_w, gn_b, dup, iup, vup, *, batch, seq, width, vres, units):
    m = pm.shape[0]
    nc = seq // WKV_CHUNK
    tc = units * LANES
    ncb = width // tc
    col = lambda c0: pl.BlockSpec((WKV_CHUNK, tc), lambda b, j, c: (b * nc + c, c0 * ncb + j))
    vec = pl.BlockSpec((1, tc), lambda b, j, c: (0, j))
    up = pl.BlockSpec((LORA_PAD, tc), lambda b, j, c: (0, j))
    row = lambda a: a.reshape(1, width)
    return pl.pallas_call(
        functools.partial(_wkv_kernel, vres=vres),
        grid=(batch, ncb, nc),
        in_specs=[col(0), col(1), col(2), col(3),
                  pl.BlockSpec((WKV_CHUNK, lora.shape[1]), lambda b, j, c: (b * nc + c, 0)), col(2),
                  vec, vec, vec, vec, vec, vec, vec, vec, up, up, up],
        out_specs=pl.BlockSpec((WKV_CHUNK, tc), lambda b, j, c: (b * nc + c, j)),
        out_shape=jax.ShapeDtypeStruct((m, width), BF16),
        scratch_shapes=[pltpu.VMEM((units, 2 * WKV_CHUNK, 2 * WKV_CHUNK), F32)],
        compiler_params=_params(("parallel", "parallel", "arbitrary")),
        name="wkv7_chunked",
    )(pm, pm, pm, pm, lora, pm_first, row(w0), row(a0), row(k_k), row(k_a), row(r_k), row(v0), row(gn_w), row(gn_b),
      dup, iup, vup)


def _tiles(m, seq):
    return dict(tm=min(1024, seq), tn=min(1024, 4 * DA_VALUE_DIM), tr=min(512, seq), tq=min(512, seq))


def _attention_layer(x, cos, sin, norm_w, w_in, q_gain, k_gain, lq1, lk1, lq2, lk2, subln_w, w_out, lambda_init,
                     *, batch, seq, t):
    d = x.shape[1]
    heads = d // DA_VALUE_DIM
    proj = _attn_in_proj(x, norm_w, w_in.astype(BF16), cos, sin, q_gain, k_gain, tm=t["tm"], tn=min(t["tn"], d))
    o = _flash_diff_attention(proj, lq1, lk1, lq2, lk2, subln_w, batch=batch, seq=seq, heads=heads,
                              lambda_init=lambda_init, tq=t["tq"])
    return _matmul_residual(o, w_out.astype(BF16), x, tm=t["tm"], tn=min(t["tn"], d))


def _pad_rows(w, rows):
    return jnp.pad(w, ((0, rows - w.shape[0]), (0, 0)))


def _pad_cols(w, cols):
    return jnp.pad(w, ((0, 0), (0, cols - w.shape[1])))


def _rwkv_layer(x, pm_first, norm_w, w_in, mu, w0, decay_up, a0, iclr_up, k_k, k_a, r_k, gn_w, gn_b, w_out, vres,
                *, batch, seq, t):
    d = x.shape[1]
    wide = 4 * d
    dr = decay_up.shape[0]
    ir = iclr_up.shape[0]
    lora_w = [_pad_cols(w_in[:, wide:wide + dr], LORA_PAD), _pad_cols(w_in[:, wide + dr:wide + dr + ir], LORA_PAD)]
    lora_mu = [jnp.pad(mu[wide:wide + dr], (0, LORA_PAD - dr)), jnp.pad(mu[wide + dr:wide + dr + ir], (0, LORA_PAD - ir))]
    if vres is not None:
        vd_w, vd_mu, v0, vu_w = vres
        lora_w.append(_pad_cols(vd_w, LORA_PAD))
        lora_mu.append(jnp.pad(vd_mu, (0, LORA_PAD - vd_mu.shape[0])))
        vup = _pad_rows(vu_w, LORA_PAD).astype(BF16)
    else:
        v0 = jnp.zeros((d,), F32)
        vup = jnp.zeros((LORA_PAD, d), BF16)
    lora_w = jnp.concatenate(lora_w, axis=1).astype(BF16)
    lora_mu = jnp.concatenate(lora_mu)
    pm = _rwkv_in_proj(x, norm_w, w_in.astype(BF16), mu[:wide], rows_per_seq=seq, tm=t["tm"], tn=t["tn"])
    lora = _rwkv_in_proj(x, norm_w, lora_w, lora_mu, rows_per_seq=seq, tm=t["tm"], tn=lora_w.shape[1])
    o = _wkv(pm, lora, pm if pm_first is None else pm_first, w0, a0, k_k, k_a, r_k.reshape(-1), v0, gn_w, gn_b,
             _pad_rows(decay_up, LORA_PAD).astype(BF16), _pad_rows(iclr_up, LORA_PAD).astype(BF16), vup,
             batch=batch, seq=seq, width=d, vres=vres is not None, units=min(WKV_UNITS, d // LANES))
    return _matmul_residual(o, w_out.astype(BF16), x, tm=t["tm"], tn=min(t["tn"], d)), pm


def kernel(x, positions, norm_w, da_w_in, da_q_gain, da_k_gain, da_lam_q1, da_lam_k1, da_lam_q2, da_lam_k2, da_subln_w, da_w_out, rw_w_in, rw_mu, rw_w0, rw_decay_up, rw_a0, rw_iclr_up, rw_k_k, rw_k_a, rw_r_k, rw_gn_w, rw_gn_b, rw_w_out, rw_vres_down, rw_vres_mu, rw_v0, rw_vres_up):
    batch, seq, d = x.shape
    depth = norm_w.shape[0]
    m = batch * seq
    t = _tiles(m, seq)
    xf = x.reshape(m, d)
    cos, sin = _rope_tables(positions.reshape(m), tr=t["tr"])
    pm_first = None
    for i in range(depth):
        j = i // 2
        if i % 2 == 0:
            lambda_init = 0.8 - 0.6 * math.exp(-0.3 * i)
            xf = _attention_layer(xf, cos, sin, norm_w[i], da_w_in[j], da_q_gain[j], da_k_gain[j], da_lam_q1[j],
                                  da_lam_k1[j], da_lam_q2[j], da_lam_k2[j], da_subln_w[j], da_w_out[j], lambda_init,
                                  batch=batch, seq=seq, t=t)
        else:
            vres = None if j == 0 else (rw_vres_down[j - 1], rw_vres_mu[j - 1], rw_v0[j - 1], rw_vres_up[j - 1])
            xf, pm_now = _rwkv_layer(xf, pm_first, norm_w[i], rw_w_in[j], rw_mu[j], rw_w0[j], rw_decay_up[j], rw_a0[j],
                                     rw_iclr_up[j], rw_k_k[j], rw_k_a[j], rw_r_k[j], rw_gn_w[j], rw_gn_b[j],
                                     rw_w_out[j], vres, batch=batch, seq=seq, t=t)
            if pm_first is None:
                pm_first = pm_now
    return xf.reshape(batch, seq, d)
```

```python
import functools
import math

import jax
import jax.numpy as jnp
from jax import lax
from jax.experimental import pallas as pl
from jax.experimental.pallas import tpu as pltpu

F32 = jnp.float32
BF16 = jnp.bfloat16

LANES = 128
DA_HEAD_DIM = 128
DA_VALUE_DIM = 2 * DA_HEAD_DIM
RW_HEAD_DIM = 64
WKV_CHUNK = 64
FLASH_GRANULES = 4
WKV_GROUPS = 16
WKV_SEQS = 2
LORA_PAD = 128
ROPE_THETA = 10000.0
NORM_EPS = 1e-6
GN_EPS = 64e-5
NEG_BIG = -1e30
VMEM_LIMIT = 56 * 1024 * 1024


def _params(sem):
    return pltpu.CompilerParams(dimension_semantics=sem, vmem_limit_bytes=VMEM_LIMIT)


def _dot(a, b):
    return jnp.dot(a, b, preferred_element_type=F32)


def _dot_nt(a, b):
    return lax.dot_general(a, b, (((1,), (1,)), ((), ())), preferred_element_type=F32)


def _split2(x):
    hi = x.astype(BF16)
    lo = (x - hi.astype(F32)).astype(BF16)
    return hi, lo


def _mm(a, b, nt=False):
    d = _dot_nt if nt else _dot
    return d(a.astype(BF16), b.astype(BF16))


def _sigmoid(x):
    return 1.0 / (1.0 + jnp.exp(-x))


def _store_normed(x_ref, g_ref, xn_ref):
    x = x_ref[...]
    ms = jnp.mean(x * x, axis=-1, keepdims=True)
    xn_ref[...] = (x * lax.rsqrt(ms + NORM_EPS) * g_ref[...]).astype(BF16)


def _rwkv_in_kernel(x_ref, g_ref, w_ref, mu_ref, o_ref, xn_ref, prev_ref, *, rows_per_seq):
    i = pl.program_id(0)
    j = pl.program_id(1)
    tm = x_ref.shape[0]

    @pl.when(j == 0)
    def _():
        _store_normed(x_ref, g_ref, xn_ref)

    @pl.when(i == 0)
    def _():
        prev_ref[j] = jnp.zeros(prev_ref.shape[1:], F32)

    acc = _dot(xn_ref[...], w_ref[...])
    seq_start = (i * tm) % rows_per_seq == 0
    prev = jnp.where(seq_start, 0.0, prev_ref[j])
    row = lax.broadcasted_iota(jnp.int32, acc.shape, 0)
    shifted = jnp.where(row == 0, prev, pltpu.roll(acc, 1, axis=0))
    prev_ref[j] = acc[tm - 1:tm, :]
    o_ref[...] = acc + (shifted - acc) * mu_ref[...]


def _rwkv_in_proj(x, gain, w_bf16, mu, *, rows_per_seq, tm, tn):
    m, d = x.shape
    n = mu.shape[0]
    assert m % tm == 0 and n % tn == 0 and rows_per_seq % tm == 0 and n <= w_bf16.shape[1]
    return pl.pallas_call(
        functools.partial(_rwkv_in_kernel, rows_per_seq=rows_per_seq),
        grid=(m // tm, n // tn),
        in_specs=[
            pl.BlockSpec((tm, d), lambda i, j: (i, 0)),
            pl.BlockSpec((1, d), lambda i, j: (0, 0)),
            pl.BlockSpec((d, tn), lambda i, j: (0, j)),
            pl.BlockSpec((1, tn), lambda i, j: (0, j)),
        ],
        out_specs=pl.BlockSpec((tm, tn), lambda i, j: (i, j)),
        out_shape=jax.ShapeDtypeStruct((m, n), F32),
        scratch_shapes=[pltpu.VMEM((tm, d), BF16), pltpu.VMEM((n // tn, 1, tn), F32)],
        compiler_params=_params(("arbitrary", "arbitrary")),
        name="rwkv_in_proj",
    )(x, gain.reshape(1, d), w_bf16, mu.reshape(1, n))


def _attn_in_kernel(x_ref, g_ref, w_ref, cos_ref, sin_ref, qg_ref, kg_ref, o_ref, xn_ref, acc_ref, *, sec, scale):
    j = pl.program_id(1)
    last = pl.num_programs(1) - 1

    def matmul():
        acc_ref[...] = _dot(xn_ref[...], w_ref[...])

    def finish_qk():
        is_q = j - 1 < sec
        gain = jnp.where(is_q, qg_ref[...], kg_ref[...])
        mult = jnp.where(is_q, scale, 1.0)
        cos = cos_ref[...]
        lane = lax.broadcasted_iota(jnp.int32, cos.shape, 1)
        sin_signed = jnp.where(lane < DA_HEAD_DIM // 2, -sin_ref[...], sin_ref[...])
        for c in range(acc_ref.shape[1] // DA_HEAD_DIM):
            sl = slice(c * DA_HEAD_DIM, (c + 1) * DA_HEAD_DIM)
            x = acc_ref[:, sl]
            y = x * lax.rsqrt(jnp.mean(x * x, axis=-1, keepdims=True) + NORM_EPS) * gain
            rot = pltpu.roll(y, DA_HEAD_DIM // 2, axis=1)
            o_ref[:, sl] = ((y * cos + rot * sin_signed) * mult).astype(BF16)

    def finish_plain():
        o_ref[...] = acc_ref[...].astype(BF16)

    @pl.when(j == 0)
    def _():
        _store_normed(x_ref, g_ref, xn_ref)
        matmul()

    @pl.when(jnp.logical_and(j >= 1, j <= 2 * sec))
    def _():
        finish_qk()
        matmul()

    @pl.when(jnp.logical_and(j > 2 * sec, j < last))
    def _():
        finish_plain()
        matmul()

    @pl.when(j == last)
    def _():
        finish_plain()


def _attn_in_proj(x, gain, w_bf16, cos, sin, q_gain, k_gain, *, tm, tn):
    m, d = x.shape
    n = w_bf16.shape[1]
    nj = n // tn
    assert m % tm == 0 and d % tn == 0 and n == 4 * d
    rope = pl.BlockSpec((tm, DA_HEAD_DIM), lambda i, j: (i, 0))
    vec = pl.BlockSpec((1, DA_HEAD_DIM), lambda i, j: (0, 0))
    return pl.pallas_call(
        functools.partial(_attn_in_kernel, sec=d // tn, scale=DA_HEAD_DIM ** -0.5 * math.log2(math.e)),
        grid=(m // tm, nj + 1),
        in_specs=[
            pl.BlockSpec((tm, d), lambda i, j: (i, 0)),
            pl.BlockSpec((1, d), lambda i, j: (0, 0)),
            pl.BlockSpec((d, tn), lambda i, j: (0, jnp.minimum(j, nj - 1))),
            rope, rope, vec, vec,
        ],
        out_specs=pl.BlockSpec((tm, tn), lambda i, j: (i, jnp.maximum(j - 1, 0))),
        out_shape=jax.ShapeDtypeStruct((m, n), BF16),
        scratch_shapes=[pltpu.VMEM((tm, d), BF16), pltpu.VMEM((tm, tn), F32)],
        compiler_params=_params(("parallel", "arbitrary")),
        name="attn_in_proj",
    )(x, gain.reshape(1, d), w_bf16, cos, sin, q_gain.reshape(1, -1), k_gain.reshape(1, -1))


def _mm_res_kernel(a_ref, w_ref, x_ref, o_ref):
    o_ref[...] = x_ref[...] + _dot(a_ref[...], w_ref[...])


def _matmul_residual(a_bf16, w_bf16, x, *, tm, tn):
    m, k = a_bf16.shape
    n = w_bf16.shape[1]
    assert m % tm == 0 and n % tn == 0
    return pl.pallas_call(
        _mm_res_kernel,
        grid=(m // tm, n // tn),
        in_specs=[
            pl.BlockSpec((tm, k), lambda i, j: (i, 0)),
            pl.BlockSpec((k, tn), lambda i, j: (0, j)),
            pl.BlockSpec((tm, tn), lambda i, j: (i, j)),
        ],
        out_specs=pl.BlockSpec((tm, tn), lambda i, j: (i, j)),
        out_shape=jax.ShapeDtypeStruct((m, n), F32),
        compiler_params=_params(("parallel", "arbitrary")),
        name="out_proj_residual",
    )(a_bf16, w_bf16, x)


def _rope_kernel(pos_ref, inv_ref, cos_ref, sin_ref):
    ang = pos_ref[...].astype(F32) * inv_ref[...]
    cos_ref[...] = jnp.cos(ang)
    sin_ref[...] = jnp.sin(ang)


def _rope_tables(positions_flat, *, tr):
    m = positions_flat.shape[0]
    inv = ROPE_THETA ** (-jnp.arange(0, DA_HEAD_DIM, 2, dtype=F32) / DA_HEAD_DIM)
    inv = jnp.concatenate([inv, inv]).reshape(1, DA_HEAD_DIM)
    return pl.pallas_call(
        _rope_kernel,
        grid=(m // tr,),
        in_specs=[pl.BlockSpec((tr, 1), lambda i: (i, 0)), pl.BlockSpec((1, DA_HEAD_DIM), lambda i: (0, 0))],
        out_specs=[pl.BlockSpec((tr, DA_HEAD_DIM), lambda i: (i, 0))] * 2,
        out_shape=[jax.ShapeDtypeStruct((m, DA_HEAD_DIM), F32)] * 2,
        compiler_params=_params(("parallel",)),
        name="rope_tables",
    )(positions_flat.reshape(m, 1), inv)


def _flash_kernel(qtab_ref, ktab_ref, kind_ref, q_ref, k_ref, v_ref, gate_ref, lq1_ref, lk1_ref, lq2_ref, lk2_ref,
                  sub_ref, o_ref, m_ref, l_ref, acc_ref, *, lambda_init):
    step_id = pl.program_id(2)
    kb = ktab_ref[step_id]
    kind = kind_ref[step_id]
    tq = q_ref.shape[0]
    per_block = k_ref.shape[0] // tq
    comps = range(2)
    sls = [slice(c * DA_HEAD_DIM, (c + 1) * DA_HEAD_DIM) for c in comps]

    @pl.when(kb == 0)
    def _():
        m_ref[...] = jnp.full(m_ref.shape, NEG_BIG, F32)
        l_ref[...] = jnp.zeros(l_ref.shape, F32)
        acc_ref[...] = jnp.zeros(acc_ref.shape, F32)

    def process(granules, diag):
        keys = granules * tq
        reps = keys // LANES
        s = [_dot_nt(q_ref[:, sl], k_ref[0:keys, sl]) for sl in sls]
        if diag:
            row = lax.broadcasted_iota(jnp.int32, (tq, tq), 0)
            col = lax.broadcasted_iota(jnp.int32, (tq, tq), 1)
            keep = col <= row
            s = [jnp.concatenate([x[:, :keys - tq], jnp.where(keep, x[:, keys - tq:], NEG_BIG)], axis=1)
                 if keys > tq else jnp.where(keep, x, NEG_BIG) for x in s]
        v = v_ref[0:keys, :]
        m, l, acc = [], [], []
        for c in comps:
            m_new = jnp.maximum(m_ref[c], jnp.max(s[c], axis=-1, keepdims=True))
            alpha = jnp.exp2(m_ref[c] - m_new)
            p = jnp.exp2(s[c] - jnp.concatenate([m_new] * reps, axis=1))
            part = p[:, 0:LANES]
            for j in range(1, reps):
                part = part + p[:, j * LANES:(j + 1) * LANES]
            m.append(m_new)
            l.append(alpha * l_ref[c] + part)
            acc.append(jnp.concatenate([alpha] * (DA_VALUE_DIM // LANES), axis=1) * acc_ref[c] + _dot(p.astype(BF16), v))
        return m, l, acc

    def carry(m, l, acc):
        for c in comps:
            m_ref[c] = m[c]
            l_ref[c] = l[c]
            acc_ref[c] = acc[c]

    def finish(l, acc):
        lam = (jnp.exp(jnp.sum(lq1_ref[...] * lk1_ref[...], axis=-1, keepdims=True))
               - jnp.exp(jnp.sum(lq2_ref[...] * lk2_ref[...], axis=-1, keepdims=True)) + lambda_init)
        l0 = jnp.sum(l[0], axis=-1, keepdims=True)
        l1 = jnp.sum(l[1], axis=-1, keepdims=True)
        o = acc[0] / l0 - lam * (acc[1] / l1)
        o = o * lax.rsqrt(jnp.mean(o * o, axis=-1, keepdims=True) + NORM_EPS) * sub_ref[...]
        g = gate_ref[...].astype(F32)
        o_ref[...] = (o * (1.0 - lambda_init) * (g * _sigmoid(g))).astype(BF16)

    @pl.when(kind == 0)
    def _():
        carry(*process(per_block, False))

    for n in range(1, per_block + 1):
        @pl.when(kind == n)
        def _(n=n):
            _, l, acc = process(n, True)
            finish(l, acc)


def _flash_diff_attention(proj, lam_q1, lam_k1, lam_q2, lam_k2, subln_w, *, batch, seq, heads,
                          lambda_init, tq):
    m = proj.shape[0]
    nq = seq // tq
    per = min(FLASH_GRANULES, nq)
    nkb = nq // per
    assert nq % per == 0
    steps = [(a, b, 0 if b < a // per else 1 + a % per) for a in range(nq) for b in range(a // per + 1)]
    qtab, ktab, kinds = (jnp.asarray(col, jnp.int32) for col in zip(*steps))
    q_blk = lambda c0: pl.BlockSpec((tq, DA_VALUE_DIM),
                                    lambda b, h, s, qt, kt, kd: (b * nq + qt[s], c0 * heads + h))
    k_blk = lambda c0: pl.BlockSpec((per * tq, DA_VALUE_DIM),
                                    lambda b, h, s, qt, kt, kd: (b * nkb + kt[s], c0 * heads + h))
    vec = lambda n: pl.BlockSpec((1, n), lambda b, h, s, qt, kt, kd: (0, 0))
    kern = functools.partial(_flash_kernel, lambda_init=lambda_init)
    grid_spec = pltpu.PrefetchScalarGridSpec(
        num_scalar_prefetch=3,
        grid=(batch, heads, len(steps)),
        in_specs=[
            q_blk(0), k_blk(1), k_blk(2), q_blk(3),
            vec(DA_HEAD_DIM), vec(DA_HEAD_DIM), vec(DA_HEAD_DIM), vec(DA_HEAD_DIM), vec(DA_VALUE_DIM),
        ],
        out_specs=q_blk(0),
        scratch_shapes=[pltpu.VMEM((2, tq, LANES), F32), pltpu.VMEM((2, tq, LANES), F32),
                        pltpu.VMEM((2, tq, DA_VALUE_DIM), F32)],
    )
    return pl.pallas_call(
        kern,
        grid_spec=grid_spec,
        out_shape=jax.ShapeDtypeStruct((m, heads * DA_VALUE_DIM), BF16),
        compiler_params=_params(("parallel", "parallel", "arbitrary")),
        name="flash_diff_attention",
    )(qtab, ktab, kinds, proj, proj, proj, proj, lam_q1.reshape(1, -1), lam_k1.reshape(1, -1),
      lam_q2.reshape(1, -1), lam_k2.reshape(1, -1), subln_w.reshape(1, -1))


def _unit_lower_inverse(lows, row, col):
    blk = lambda sh: (row >> sh) == (col >> sh)
    eye = jnp.where(row == col, 1.0, 0.0)
    d1 = [jnp.where(blk(3), lo, 0.0) for lo in lows]
    d1b = [d.astype(BF16) for d in d1]
    d2 = [_mm(d, d) for d in d1b]
    d2b = [d.astype(BF16) for d in d2]
    d4 = [_mm(d, d) for d in d2b]
    xs = [eye + a + b + _mm(ab, bb) for a, b, ab, bb in zip(d1, d2, d1b, d2b)]
    xs = [x + _mm(x, d) for x, d in zip(xs, d4)]
    for sh in (3, 4, 5):
        off = jnp.logical_and(blk(sh + 1), jnp.logical_not(blk(sh)))
        ns = [jnp.where(off, lo, 0.0) for lo in lows]
        xb = [x.astype(BF16) for x in xs]
        ts = [_mm(x, n) for x, n in zip(xb, ns)]
        xs = [x + _mm(t, y) for x, t, y in zip(xs, ts, xb)]
    return xs


def _wkv_kernel(r_ref, k_ref, v_ref, g_ref, lora_ref, vfirst_ref, w0_ref, a0_ref, kk_ref, ka_ref, rk_ref, v0_ref,
                gnw_ref, gnb_ref, dup_ref, iup_ref, vup_ref, o_ref, s_ref, *, vres):
    c = pl.program_id(2)
    seqs = range(r_ref.shape[0])
    groups = r_ref.shape[2] // LANES
    ch = WKV_CHUNK
    n2 = 2 * ch

    @pl.when(c == 0)
    def _():
        s_ref[...] = jnp.zeros(s_ref.shape, F32)

    row = lax.broadcasted_iota(jnp.int32, (n2, n2), 0)
    col = lax.broadcasted_iota(jnp.int32, (n2, n2), 1)
    strict = col < row
    incl = col <= row
    lane = lax.broadcasted_iota(jnp.int32, (ch, LANES), 1)
    first = lane < RW_HEAD_DIM
    m0 = jnp.where(first, 1.0, 0.0)
    m1 = 1.0 - m0
    r64 = lax.broadcasted_iota(jnp.int32, (ch, ch), 0)
    c64 = lax.broadcasted_iota(jnp.int32, (ch, ch), 1)
    tri = jnp.where(c64 <= r64, 1.0, 0.0).astype(BF16)
    sls = [slice(u * LANES, (u + 1) * LANES) for u in range(groups)]

    def segsum(x):
        s0 = jnp.sum(x * m0, axis=-1, keepdims=True)
        s1 = jnp.sum(x * m1, axis=-1, keepdims=True)
        return jnp.where(first, s0, s1)

    def stack2(x):
        return jnp.concatenate([jnp.where(first, x, 0), jnp.where(first, 0, x)], axis=0)

    a2, b2, k2, r2, v2, g_end, bonus_in, gate = [], [], [], [], [], [], [], []
    for bi in seqs:
        lora = lora_ref[bi]
        dwt = jnp.tanh(lora[:, 0:LORA_PAD]).astype(BF16)
        da = lora[:, LORA_PAD:2 * LORA_PAD].astype(BF16)
        lw = -math.exp(-0.5) * _sigmoid(w0_ref[...] + _dot(dwt, dup_ref[...]))
        a = _sigmoid(a0_ref[...] + _dot(da, iup_ref[...]))
        r = r_ref[bi]
        k = k_ref[bi]
        v = v_ref[bi]
        if vres:
            pv = lora[:, 2 * LORA_PAD:3 * LORA_PAD].astype(BF16)
            v = v + (vfirst_ref[bi] - v) * _sigmoid(v0_ref[...] + _dot(pv, vup_ref[...]))
        kk = k * kk_ref[...]
        k_mod = k * (1.0 + (a - 1.0) * ka_ref[...])
        rkr = r * k_mod * rk_ref[...]
        lw_hi, lw_lo = _split2(lw)
        cum = _dot(tri, lw_hi) + _dot(tri, lw_lo)
        g_in = jnp.exp(cum)
        g_inv = jnp.exp(-cum)
        g_ex = jnp.exp(cum - lw)
        rt = (r * g_in).astype(BF16)
        kt = (k_mod * g_inv).astype(BF16)
        vb = v.astype(BF16)
        for sl in sls:
            kc = kk[:, sl]
            kn = kc * lax.rsqrt(jnp.maximum(segsum(kc * kc), 1e-24))
            a2.append(stack2((-kn * g_ex[:, sl]).astype(BF16)))
            b2.append(stack2((kn * a[:, sl] * g_inv[:, sl]).astype(BF16)))
            k2.append(stack2(kt[:, sl]))
            r2.append(stack2(rt[:, sl]))
            v2.append(stack2(vb[:, sl]))
            g_end.append(g_in[ch - 1:ch, sl])
            bonus_in.append((rkr[:, sl], v[:, sl]))
            gate.append(g_ref[bi, :, sl])
    units = range(len(a2))

    bk = [jnp.concatenate([b, k], axis=0) for b, k in zip(b2, k2)]
    sc = [_mm(jnp.concatenate([a, r], axis=0), x, nt=True) for a, r, x in zip(a2, r2, bk)]
    low_ab = [jnp.where(strict, s[:n2, :n2], 0.0) for s in sc]
    low_ak = [jnp.where(strict, s[:n2, n2:], 0.0) for s in sc]
    low_rb = [jnp.where(incl, s[n2:, :n2], 0.0) for s in sc]
    low_rk = [jnp.where(incl, s[n2:, n2:], 0.0) for s in sc]

    w1 = [_mm(l, v) for l, v in zip(low_ak, v2)]
    rkv = [_mm(l, v) for l, v in zip(low_rk, v2)]
    tinv = _unit_lower_inverse(low_ab, row, col)
    au = [_mm(t, jnp.concatenate([a, w.astype(BF16)], axis=1)) for t, a, w in zip(tinv, a2, w1)]

    ss = [s_ref[u] for u in units]
    ars = [_mm(jnp.concatenate([x[:, :n2].astype(BF16), r], axis=0), s, nt=True) for x, r, s in zip(au, r2, ss)]
    us = [x[:n2] + y[:, n2:] for x, y in zip(ars, au)]
    y2 = [x[n2:] + _mm(l, u) + z for x, l, u, z in zip(ars, low_rb, us, rkv)]
    upd = [_mm(jnp.concatenate([u, v.astype(F32)], axis=0).T, y) for u, v, y in zip(us, v2, bk)]
    for u in units:
        s_ref[u] = (ss[u] + upd[u]) * g_end[u]

    inv_n = 1.0 / RW_HEAD_DIM
    ys = [y[:ch] + y[ch:] for y in y2]
    ds = [y - segsum(y) * inv_n for y in ys]
    var = [segsum(d * d) * inv_n for d in ds]
    for u in units:
        bi, sl = u // groups, sls[u % groups]
        yn = ds[u] * lax.rsqrt(var[u] + GN_EPS) * gnw_ref[:, sl] + gnb_ref[:, sl]
        rkr, v = bonus_in[u]
        g = gate[u]
        o_ref[bi, :, sl] = ((yn + segsum(rkr) * v) * (g * _sigmoid(g))).astype(BF16)


def _wkv(pm, lora, pm_first, w0, a0, k_k, k_a, r_k, v0, gn_w, gn_b, dup, iup, vup, *, batch, seq, width, vres):
    m = pm.shape[0]
    nc = seq // WKV_CHUNK
    tc = min(WKV_GROUPS * LANES, width)
    ncb = width // tc
    nb = math.gcd(WKV_SEQS, batch)
    view = lambda a: a.reshape(batch, seq, a.shape[1])
    col = lambda c0: pl.BlockSpec((nb, WKV_CHUNK, tc), lambda b, j, c: (b, c, c0 * ncb + j))
    vec = pl.BlockSpec((1, tc), lambda b, j, c: (0, j))
    up = pl.BlockSpec((LORA_PAD, tc), lambda b, j, c: (0, j))
    row = lambda a: a.reshape(1, width)
    out = pl.pallas_call(
        functools.partial(_wkv_kernel, vres=vres),
        grid=(batch // nb, ncb, nc),
        in_specs=[col(0), col(1), col(2), col(3),
                  pl.BlockSpec((nb, WKV_CHUNK, lora.shape[1]), lambda b, j, c: (b, c, 0)), col(2),
                  vec, vec, vec, vec, vec, vec, vec, vec, up, up, up],
        out_specs=pl.BlockSpec((nb, WKV_CHUNK, tc), lambda b, j, c: (b, c, j)),
        out_shape=jax.ShapeDtypeStruct((batch, seq, width), BF16),
        scratch_shapes=[pltpu.VMEM((nb * tc // LANES, 2 * WKV_CHUNK, 2 * WKV_CHUNK), F32)],
        compiler_params=_params(("parallel", "parallel", "arbitrary")),
        name="wkv7_chunked",
    )(view(pm), view(pm), view(pm), view(pm), view(lora), view(pm_first), row(w0), row(a0), row(k_k), row(k_a),
      row(r_k), row(v0), row(gn_w), row(gn_b), dup, iup, vup)
    return out.reshape(m, width)


def _tiles(m, seq):
    return dict(tm=min(1024, seq), tn=min(1024, 4 * DA_VALUE_DIM), tr=min(512, seq), tq=min(512, seq))


def _attention_layer(x, cos, sin, norm_w, w_in, q_gain, k_gain, lq1, lk1, lq2, lk2, subln_w, w_out, lambda_init,
                     *, batch, seq, t):
    d = x.shape[1]
    heads = d // DA_VALUE_DIM
    proj = _attn_in_proj(x, norm_w, w_in.astype(BF16), cos, sin, q_gain, k_gain, tm=t["tm"], tn=min(t["tn"], d))
    o = _flash_diff_attention(proj, lq1, lk1, lq2, lk2, subln_w, batch=batch, seq=seq, heads=heads,
                              lambda_init=lambda_init, tq=t["tq"])
    return _matmul_residual(o, w_out.astype(BF16), x, tm=t["tm"], tn=min(t["tn"], d))


def _pad_rows(w, rows):
    return jnp.pad(w, ((0, rows - w.shape[0]), (0, 0)))


def _pad_cols(w, cols):
    return jnp.pad(w, ((0, 0), (0, cols - w.shape[1])))


def _rwkv_layer(x, pm_first, norm_w, w_in, mu, w0, decay_up, a0, iclr_up, k_k, k_a, r_k, gn_w, gn_b, w_out, vres,
                *, batch, seq, t):
    d = x.shape[1]
    wide = 4 * d
    dr = decay_up.shape[0]
    ir = iclr_up.shape[0]
    lora_w = [_pad_cols(w_in[:, wide:wide + dr], LORA_PAD), _pad_cols(w_in[:, wide + dr:wide + dr + ir], LORA_PAD)]
    lora_mu = [jnp.pad(mu[wide:wide + dr], (0, LORA_PAD - dr)), jnp.pad(mu[wide + dr:wide + dr + ir], (0, LORA_PAD - ir))]
    if vres is not None:
        vd_w, vd_mu, v0, vu_w = vres
        lora_w.append(_pad_cols(vd_w, LORA_PAD))
        lora_mu.append(jnp.pad(vd_mu, (0, LORA_PAD - vd_mu.shape[0])))
        vup = _pad_rows(vu_w, LORA_PAD).astype(BF16)
    else:
        v0 = jnp.zeros((d,), F32)
        vup = jnp.zeros((LORA_PAD, d), BF16)
    lora_w = jnp.concatenate(lora_w, axis=1).astype(BF16)
    lora_mu = jnp.concatenate(lora_mu)
    pm = _rwkv_in_proj(x, norm_w, w_in.astype(BF16), mu[:wide], rows_per_seq=seq, tm=t["tm"], tn=t["tn"])
    lora = _rwkv_in_proj(x, norm_w, lora_w, lora_mu, rows_per_seq=seq, tm=t["tm"], tn=lora_w.shape[1])
    o = _wkv(pm, lora, pm if pm_first is None else pm_first, w0, a0, k_k, k_a, r_k.reshape(-1), v0, gn_w, gn_b,
             _pad_rows(decay_up, LORA_PAD).astype(BF16), _pad_rows(iclr_up, LORA_PAD).astype(BF16), vup,
             batch=batch, seq=seq, width=d, vres=vres is not None)
    return _matmul_residual(o, w_out.astype(BF16), x, tm=t["tm"], tn=min(t["tn"], d)), pm


def kernel(x, positions, norm_w, da_w_in, da_q_gain, da_k_gain, da_lam_q1, da_lam_k1, da_lam_q2, da_lam_k2, da_subln_w, da_w_out, rw_w_in, rw_mu, rw_w0, rw_decay_up, rw_a0, rw_iclr_up, rw_k_k, rw_k_a, rw_r_k, rw_gn_w, rw_gn_b, rw_w_out, rw_vres_down, rw_vres_mu, rw_v0, rw_vres_up):
    batch, seq, d = x.shape
    depth = norm_w.shape[0]
    m = batch * seq
    t = _tiles(m, seq)
    xf = x.reshape(m, d)
    cos, sin = _rope_tables(positions.reshape(m), tr=t["tr"])
    pm_first = None
    for i in range(depth):
        j = i // 2
        if i % 2 == 0:
            lambda_init = 0.8 - 0.6 * math.exp(-0.3 * i)
            xf = _attention_layer(xf, cos, sin, norm_w[i], da_w_in[j], da_q_gain[j], da_k_gain[j], da_lam_q1[j],
                                  da_lam_k1[j], da_lam_q2[j], da_lam_k2[j], da_subln_w[j], da_w_out[j], lambda_init,
                                  batch=batch, seq=seq, t=t)
        else:
            vres = None if j == 0 else (rw_vres_down[j - 1], rw_vres_mu[j - 1], rw_v0[j - 1], rw_vres_up[j - 1])
            xf, pm_now = _rwkv_layer(xf, pm_first, norm_w[i], rw_w_in[j], rw_mu[j], rw_w0[j], rw_decay_up[j], rw_a0[j],
                                     rw_iclr_up[j], rw_k_k[j], rw_k_a[j], rw_r_k[j], rw_gn_w[j], rw_gn_b[j],
                                     rw_w_out[j], vres, batch=batch, seq=seq, t=t)
            if pm_first is None:
                pm_first = pm_now
    return xf.reshape(batch, seq, d)
```

```python
import functools
import math

import jax
import jax.numpy as jnp
from jax import lax
from jax.experimental import pallas as pl
from jax.experimental.pallas import tpu as pltpu

F32 = jnp.float32
BF16 = jnp.bfloat16

LANES = 128
DA_HEAD_DIM = 128
DA_VALUE_DIM = 2 * DA_HEAD_DIM
RW_HEAD_DIM = 64
WKV_CHUNK = 64
FLASH_GRANULES = 4
WKV_GROUPS = 16
WKV_SEQS = 2
LORA_PAD = 128
ROPE_THETA = 10000.0
NORM_EPS = 1e-6
GN_EPS = 64e-5
NEG_BIG = -1e30
VMEM_LIMIT = 56 * 1024 * 1024


def _params(sem):
    return pltpu.CompilerParams(dimension_semantics=sem, vmem_limit_bytes=VMEM_LIMIT)


def _dot(a, b):
    return jnp.dot(a, b, preferred_element_type=F32)


def _dot_nt(a, b):
    return lax.dot_general(a, b, (((1,), (1,)), ((), ())), preferred_element_type=F32)


def _split2(x):
    hi = x.astype(BF16)
    lo = (x - hi.astype(F32)).astype(BF16)
    return hi, lo


def _mm(a, b, nt=False):
    d = _dot_nt if nt else _dot
    return d(a.astype(BF16), b.astype(BF16))


def _sigmoid(x):
    return 1.0 / (1.0 + jnp.exp(-x))


def _store_normed(x_ref, g_ref, xn_ref):
    x = x_ref[...]
    ms = jnp.mean(x * x, axis=-1, keepdims=True)
    xn_ref[...] = (x * lax.rsqrt(ms + NORM_EPS) * g_ref[...]).astype(BF16)


def _rwkv_in_kernel(x_ref, g_ref, w_ref, mu_ref, lw_ref, lmu_ref, o_ref, lo_ref, xn_ref, prev_ref, lprev_ref,
                    *, rows_per_seq):
    i = pl.program_id(0)
    j = pl.program_id(1)
    tm = x_ref.shape[0]
    seq_start = (i * tm) % rows_per_seq == 0

    def shift_mix(acc, prev, mu):
        row = lax.broadcasted_iota(jnp.int32, acc.shape, 0)
        shifted = jnp.where(row == 0, jnp.where(seq_start, 0.0, prev), pltpu.roll(acc, 1, axis=0))
        return acc + (shifted - acc) * mu

    @pl.when(i == 0)
    def _():
        prev_ref[j] = jnp.zeros(prev_ref.shape[1:], F32)

    @pl.when(jnp.logical_and(i == 0, j == 0))
    def _():
        lprev_ref[...] = jnp.zeros(lprev_ref.shape, F32)

    @pl.when(j == 0)
    def _():
        _store_normed(x_ref, g_ref, xn_ref)
        lacc = _dot(xn_ref[...], lw_ref[...])
        lo_ref[...] = shift_mix(lacc, lprev_ref[...], lmu_ref[...])
        lprev_ref[...] = lacc[tm - 1:tm, :]

    acc = _dot(xn_ref[...], w_ref[...])
    o_ref[...] = shift_mix(acc, prev_ref[j], mu_ref[...])
    prev_ref[j] = acc[tm - 1:tm, :]


def _rwkv_in_proj(x, gain, w_bf16, mu, lora_w, lora_mu, *, rows_per_seq, tm, tn):
    m, d = x.shape
    n = mu.shape[0]
    nl = lora_w.shape[1]
    assert m % tm == 0 and n % tn == 0 and rows_per_seq % tm == 0 and n <= w_bf16.shape[1]
    return pl.pallas_call(
        functools.partial(_rwkv_in_kernel, rows_per_seq=rows_per_seq),
        grid=(m // tm, n // tn),
        in_specs=[
            pl.BlockSpec((tm, d), lambda i, j: (i, 0)),
            pl.BlockSpec((1, d), lambda i, j: (0, 0)),
            pl.BlockSpec((d, tn), lambda i, j: (0, j)),
            pl.BlockSpec((1, tn), lambda i, j: (0, j)),
            pl.BlockSpec((d, nl), lambda i, j: (0, 0)),
            pl.BlockSpec((1, nl), lambda i, j: (0, 0)),
        ],
        out_specs=[pl.BlockSpec((tm, tn), lambda i, j: (i, j)), pl.BlockSpec((tm, nl), lambda i, j: (i, 0))],
        out_shape=[jax.ShapeDtypeStruct((m, n), F32), jax.ShapeDtypeStruct((m, nl), F32)],
        scratch_shapes=[pltpu.VMEM((tm, d), BF16), pltpu.VMEM((n // tn, 1, tn), F32), pltpu.VMEM((1, nl), F32)],
        compiler_params=_params(("arbitrary", "arbitrary")),
        name="rwkv_in_proj",
    )(x, gain.reshape(1, d), w_bf16, mu.reshape(1, n), lora_w, lora_mu.reshape(1, nl))


def _attn_in_kernel(x_ref, g_ref, w_ref, cos_ref, sin_ref, qg_ref, kg_ref, o_ref, xn_ref, acc_ref, *, sec, scale):
    j = pl.program_id(1)
    last = pl.num_programs(1) - 1

    def matmul():
        acc_ref[...] = _dot(xn_ref[...], w_ref[...])

    def finish_qk():
        is_q = j - 1 < sec
        gain = jnp.where(is_q, qg_ref[...], kg_ref[...])
        mult = jnp.where(is_q, scale, 1.0)
        cos = cos_ref[...]
        lane = lax.broadcasted_iota(jnp.int32, cos.shape, 1)
        sin_signed = jnp.where(lane < DA_HEAD_DIM // 2, -sin_ref[...], sin_ref[...])
        for c in range(acc_ref.shape[1] // DA_HEAD_DIM):
            sl = slice(c * DA_HEAD_DIM, (c + 1) * DA_HEAD_DIM)
            x = acc_ref[:, sl]
            y = x * lax.rsqrt(jnp.mean(x * x, axis=-1, keepdims=True) + NORM_EPS) * gain
            rot = pltpu.roll(y, DA_HEAD_DIM // 2, axis=1)
            o_ref[:, sl] = ((y * cos + rot * sin_signed) * mult).astype(BF16)

    def finish_plain():
        o_ref[...] = acc_ref[...].astype(BF16)

    @pl.when(j == 0)
    def _():
        _store_normed(x_ref, g_ref, xn_ref)
        matmul()

    @pl.when(jnp.logical_and(j >= 1, j <= 2 * sec))
    def _():
        finish_qk()
        matmul()

    @pl.when(jnp.logical_and(j > 2 * sec, j < last))
    def _():
        finish_plain()
        matmul()

    @pl.when(j == last)
    def _():
        finish_plain()


def _attn_in_proj(x, gain, w_bf16, cos, sin, q_gain, k_gain, *, tm, tn):
    m, d = x.shape
    n = w_bf16.shape[1]
    nj = n // tn
    assert m % tm == 0 and d % tn == 0 and n == 4 * d
    rope = pl.BlockSpec((tm, DA_HEAD_DIM), lambda i, j: (i, 0))
    vec = pl.BlockSpec((1, DA_HEAD_DIM), lambda i, j: (0, 0))
    return pl.pallas_call(
        functools.partial(_attn_in_kernel, sec=d // tn, scale=DA_HEAD_DIM ** -0.5 * math.log2(math.e)),
        grid=(m // tm, nj + 1),
        in_specs=[
            pl.BlockSpec((tm, d), lambda i, j: (i, 0)),
            pl.BlockSpec((1, d), lambda i, j: (0, 0)),
            pl.BlockSpec((d, tn), lambda i, j: (0, jnp.minimum(j, nj - 1))),
            rope, rope, vec, vec,
        ],
        out_specs=pl.BlockSpec((tm, tn), lambda i, j: (i, jnp.maximum(j - 1, 0))),
        out_shape=jax.ShapeDtypeStruct((m, n), BF16),
        scratch_shapes=[pltpu.VMEM((tm, d), BF16), pltpu.VMEM((tm, tn), F32)],
        compiler_params=_params(("parallel", "arbitrary")),
        name="attn_in_proj",
    )(x, gain.reshape(1, d), w_bf16, cos, sin, q_gain.reshape(1, -1), k_gain.reshape(1, -1))


def _mm_res_kernel(a_ref, w_ref, x_ref, o_ref):
    o_ref[...] = x_ref[...] + _dot(a_ref[...], w_ref[...])


def _matmul_residual(a_bf16, w_bf16, x, *, tm, tn):
    m, k = a_bf16.shape
    n = w_bf16.shape[1]
    assert m % tm == 0 and n % tn == 0
    return pl.pallas_call(
        _mm_res_kernel,
        grid=(m // tm, n // tn),
        in_specs=[
            pl.BlockSpec((tm, k), lambda i, j: (i, 0)),
            pl.BlockSpec((k, tn), lambda i, j: (0, j)),
            pl.BlockSpec((tm, tn), lambda i, j: (i, j)),
        ],
        out_specs=pl.BlockSpec((tm, tn), lambda i, j: (i, j)),
        out_shape=jax.ShapeDtypeStruct((m, n), F32),
        compiler_params=_params(("parallel", "arbitrary")),
        name="out_proj_residual",
    )(a_bf16, w_bf16, x)


def _rope_kernel(pos_ref, inv_ref, cos_ref, sin_ref):
    ang = pos_ref[...].astype(F32) * inv_ref[...]
    cos_ref[...] = jnp.cos(ang)
    sin_ref[...] = jnp.sin(ang)


def _rope_tables(positions_flat, *, tr):
    m = positions_flat.shape[0]
    inv = ROPE_THETA ** (-jnp.arange(0, DA_HEAD_DIM, 2, dtype=F32) / DA_HEAD_DIM)
    inv = jnp.concatenate([inv, inv]).reshape(1, DA_HEAD_DIM)
    return pl.pallas_call(
        _rope_kernel,
        grid=(m // tr,),
        in_specs=[pl.BlockSpec((tr, 1), lambda i: (i, 0)), pl.BlockSpec((1, DA_HEAD_DIM), lambda i: (0, 0))],
        out_specs=[pl.BlockSpec((tr, DA_HEAD_DIM), lambda i: (i, 0))] * 2,
        out_shape=[jax.ShapeDtypeStruct((m, DA_HEAD_DIM), F32)] * 2,
        compiler_params=_params(("parallel",)),
        name="rope_tables",
    )(positions_flat.reshape(m, 1), inv)


def _flash_kernel(qtab_ref, ktab_ref, kind_ref, q_ref, k_ref, v_ref, gate_ref, lq1_ref, lk1_ref, lq2_ref, lk2_ref,
                  sub_ref, o_ref, m_ref, l_ref, acc_ref, *, lambda_init):
    step_id = pl.program_id(2)
    kb = ktab_ref[step_id]
    kind = kind_ref[step_id]
    tq = q_ref.shape[0]
    per_block = k_ref.shape[0] // tq
    comps = range(2)
    sls = [slice(c * DA_HEAD_DIM, (c + 1) * DA_HEAD_DIM) for c in comps]

    @pl.when(kb == 0)
    def _():
        m_ref[...] = jnp.full(m_ref.shape, NEG_BIG, F32)
        l_ref[...] = jnp.zeros(l_ref.shape, F32)
        acc_ref[...] = jnp.zeros(acc_ref.shape, F32)

    def process(granules, diag):
        below = (granules - 1) * tq if diag else granules * tq
        spans = ([(0, below, False)] if below else []) + ([(below, below + tq, True)] if diag else [])
        s = [[_dot_nt(q_ref[:, sl], k_ref[lo:hi, sl]) for lo, hi, _ in spans] for sl in sls]
        if diag:
            row = lax.broadcasted_iota(jnp.int32, (tq, tq), 0)
            col = lax.broadcasted_iota(jnp.int32, (tq, tq), 1)
            keep = col <= row
            s = [xs[:-1] + [jnp.where(keep, xs[-1], NEG_BIG)] for xs in s]
        m = []
        for c in comps:
            m_new = m_ref[c]
            for x in s[c]:
                m_new = jnp.maximum(m_new, jnp.max(x, axis=-1, keepdims=True))
            m.append(m_new)
        alpha = [jnp.exp2(m_ref[c] - m[c]) for c in comps]
        p = [[jnp.exp2(x - jnp.concatenate([m[c]] * ((hi - lo) // LANES), axis=1))
              for x, (lo, hi, _) in zip(s[c], spans)] for c in comps]
        pv = [[_dot(x.astype(BF16), v_ref[lo:hi, :]) for x, (lo, hi, _) in zip(p[c], spans)] for c in comps]
        l, acc = [], []
        for c in comps:
            part = alpha[c] * l_ref[c]
            for x in p[c]:
                for j in range(x.shape[1] // LANES):
                    part = part + x[:, j * LANES:(j + 1) * LANES]
            l.append(part)
            acc.append(jnp.concatenate([alpha[c]] * (DA_VALUE_DIM // LANES), axis=1) * acc_ref[c] + sum(pv[c]))
        return m, l, acc

    def carry(m, l, acc):
        for c in comps:
            m_ref[c] = m[c]
            l_ref[c] = l[c]
            acc_ref[c] = acc[c]

    def finish(l, acc):
        lam = (jnp.exp(jnp.sum(lq1_ref[...] * lk1_ref[...], axis=-1, keepdims=True))
               - jnp.exp(jnp.sum(lq2_ref[...] * lk2_ref[...], axis=-1, keepdims=True)) + lambda_init)
        l0 = jnp.sum(l[0], axis=-1, keepdims=True)
        l1 = jnp.sum(l[1], axis=-1, keepdims=True)
        o = acc[0] / l0 - lam * (acc[1] / l1)
        o = o * lax.rsqrt(jnp.mean(o * o, axis=-1, keepdims=True) + NORM_EPS) * sub_ref[...]
        g = gate_ref[...].astype(F32)
        o_ref[...] = (o * (1.0 - lambda_init) * (g * _sigmoid(g))).astype(BF16)

    @pl.when(kind == 0)
    def _():
        carry(*process(per_block, False))

    for n in range(1, per_block + 1):
        @pl.when(kind == n)
        def _(n=n):
            _, l, acc = process(n, True)
            finish(l, acc)


def _flash_diff_attention(proj, lam_q1, lam_k1, lam_q2, lam_k2, subln_w, *, batch, seq, heads,
                          lambda_init, tq):
    m = proj.shape[0]
    nq = seq // tq
    per = min(FLASH_GRANULES, nq)
    nkb = nq // per
    assert nq % per == 0
    steps = [(a, b, 0 if b < a // per else 1 + a % per) for a in range(nq) for b in range(a // per + 1)]
    qtab, ktab, kinds = (jnp.asarray(col, jnp.int32) for col in zip(*steps))
    q_blk = lambda c0: pl.BlockSpec((tq, DA_VALUE_DIM),
                                    lambda b, h, s, qt, kt, kd: (b * nq + qt[s], c0 * heads + h))
    k_blk = lambda c0: pl.BlockSpec((per * tq, DA_VALUE_DIM),
                                    lambda b, h, s, qt, kt, kd: (b * nkb + kt[s], c0 * heads + h))
    vec = lambda n: pl.BlockSpec((1, n), lambda b, h, s, qt, kt, kd: (0, 0))
    kern = functools.partial(_flash_kernel, lambda_init=lambda_init)
    grid_spec = pltpu.PrefetchScalarGridSpec(
        num_scalar_prefetch=3,
        grid=(batch, heads, len(steps)),
        in_specs=[
            q_blk(0), k_blk(1), k_blk(2), q_blk(3),
            vec(DA_HEAD_DIM), vec(DA_HEAD_DIM), vec(DA_HEAD_DIM), vec(DA_HEAD_DIM), vec(DA_VALUE_DIM),
        ],
        out_specs=q_blk(0),
        scratch_shapes=[pltpu.VMEM((2, tq, LANES), F32), pltpu.VMEM((2, tq, LANES), F32),
                        pltpu.VMEM((2, tq, DA_VALUE_DIM), F32)],
    )
    return pl.pallas_call(
        kern,
        grid_spec=grid_spec,
        out_shape=jax.ShapeDtypeStruct((m, heads * DA_VALUE_DIM), BF16),
        compiler_params=_params(("parallel", "parallel", "arbitrary")),
        name="flash_diff_attention",
    )(qtab, ktab, kinds, proj, proj, proj, proj, lam_q1.reshape(1, -1), lam_k1.reshape(1, -1),
      lam_q2.reshape(1, -1), lam_k2.reshape(1, -1), subln_w.reshape(1, -1))


def _unit_lower_inverse(lows, row, col):
    blk = lambda sh: (row >> sh) == (col >> sh)
    eye = jnp.where(row == col, 1.0, 0.0)
    d1 = [jnp.where(blk(3), lo, 0.0) for lo in lows]
    d1b = [d.astype(BF16) for d in d1]
    d2 = [_mm(d, d) for d in d1b]
    d2b = [d.astype(BF16) for d in d2]
    d4 = [_mm(d, d) for d in d2b]
    xs = [eye + a + b + _mm(ab, bb) for a, b, ab, bb in zip(d1, d2, d1b, d2b)]
    xs = [x + _mm(x, d) for x, d in zip(xs, d4)]
    for sh in (3, 4, 5):
        off = jnp.logical_and(blk(sh + 1), jnp.logical_not(blk(sh)))
        ns = [jnp.where(off, lo, 0.0) for lo in lows]
        xb = [x.astype(BF16) for x in xs]
        ts = [_mm(x, n) for x, n in zip(xb, ns)]
        xs = [x + _mm(t, y) for x, t, y in zip(xs, ts, xb)]
    return xs


def _wkv_kernel(r_ref, k_ref, v_ref, g_ref, lora_ref, vfirst_ref, w0_ref, a0_ref, kk_ref, ka_ref, rk_ref, v0_ref,
                gnw_ref, gnb_ref, dup_ref, iup_ref, vup_ref, o_ref, s_ref, *, vres):
    c = pl.program_id(2)
    seqs = range(r_ref.shape[0])
    groups = r_ref.shape[2] // LANES
    ch = WKV_CHUNK
    n2 = 2 * ch

    @pl.when(c == 0)
    def _():
        s_ref[...] = jnp.zeros(s_ref.shape, F32)

    row = lax.broadcasted_iota(jnp.int32, (n2, n2), 0)
    col = lax.broadcasted_iota(jnp.int32, (n2, n2), 1)
    strict = col < row
    incl = col <= row
    lane = lax.broadcasted_iota(jnp.int32, (ch, LANES), 1)
    first = lane < RW_HEAD_DIM
    m0 = jnp.where(first, 1.0, 0.0)
    m1 = 1.0 - m0
    r64 = lax.broadcasted_iota(jnp.int32, (ch, ch), 0)
    c64 = lax.broadcasted_iota(jnp.int32, (ch, ch), 1)
    tri = jnp.where(c64 <= r64, 1.0, 0.0).astype(BF16)
    sls = [slice(u * LANES, (u + 1) * LANES) for u in range(groups)]

    def segsum(x):
        s0 = jnp.sum(x * m0, axis=-1, keepdims=True)
        s1 = jnp.sum(x * m1, axis=-1, keepdims=True)
        return jnp.where(first, s0, s1)

    def stack2(x):
        return jnp.concatenate([jnp.where(first, x, 0), jnp.where(first, 0, x)], axis=0)

    a2, b2, k2, r2, v2, g_end, bonus_in, gate = [], [], [], [], [], [], [], []
    for bi in seqs:
        lora = lora_ref[bi]
        dwt = jnp.tanh(lora[:, 0:LORA_PAD]).astype(BF16)
        da = lora[:, LORA_PAD:2 * LORA_PAD].astype(BF16)
        lw = -math.exp(-0.5) * _sigmoid(w0_ref[...] + _dot(dwt, dup_ref[...]))
        a = _sigmoid(a0_ref[...] + _dot(da, iup_ref[...]))
        r = r_ref[bi]
        k = k_ref[bi]
        v = v_ref[bi]
        if vres:
            pv = lora[:, 2 * LORA_PAD:3 * LORA_PAD].astype(BF16)
            v = v + (vfirst_ref[bi] - v) * _sigmoid(v0_ref[...] + _dot(pv, vup_ref[...]))
        kk = k * kk_ref[...]
        k_mod = k * (1.0 + (a - 1.0) * ka_ref[...])
        rkr = r * k_mod * rk_ref[...]
        lw_hi, lw_lo = _split2(lw)
        cum = _dot(tri, lw_hi) + _dot(tri, lw_lo)
        g_in = jnp.exp(cum)
        g_inv = jnp.exp(-cum)
        g_ex = jnp.exp(cum - lw)
        rt = (r * g_in).astype(BF16)
        kt = (k_mod * g_inv).astype(BF16)
        vb = v.astype(BF16)
        for sl in sls:
            kc = kk[:, sl]
            kn = kc * lax.rsqrt(jnp.maximum(segsum(kc * kc), 1e-24))
            a2.append(stack2((-kn * g_ex[:, sl]).astype(BF16)))
            b2.append(stack2((kn * a[:, sl] * g_inv[:, sl]).astype(BF16)))
            k2.append(stack2(kt[:, sl]))
            r2.append(stack2(rt[:, sl]))
            v2.append(stack2(vb[:, sl]))
            g_end.append(g_in[ch - 1:ch, sl])
            bonus_in.append((rkr[:, sl], v[:, sl]))
            gate.append(g_ref[bi, :, sl])
    units = range(len(a2))

    bk = [jnp.concatenate([b, k], axis=0) for b, k in zip(b2, k2)]
    sc = [_mm(jnp.concatenate([a, r], axis=0), x, nt=True) for a, r, x in zip(a2, r2, bk)]
    low_ab = [jnp.where(strict, s[:n2, :n2], 0.0) for s in sc]
    low_ak = [jnp.where(strict, s[:n2, n2:], 0.0) for s in sc]
    low_rb = [jnp.where(incl, s[n2:, :n2], 0.0) for s in sc]
    low_rk = [jnp.where(incl, s[n2:, n2:], 0.0) for s in sc]

    w1 = [_mm(l, v) for l, v in zip(low_ak, v2)]
    rkv = [_mm(l, v) for l, v in zip(low_rk, v2)]
    tinv = _unit_lower_inverse(low_ab, row, col)
    au = [_mm(t, jnp.concatenate([a, w.astype(BF16)], axis=1)) for t, a, w in zip(tinv, a2, w1)]

    ss = [s_ref[u] for u in units]
    ars = [_mm(jnp.concatenate([x[:, :n2].astype(BF16), r], axis=0), s, nt=True) for x, r, s in zip(au, r2, ss)]
    us = [x[:n2] + y[:, n2:] for x, y in zip(ars, au)]
    y2 = [x[n2:] + _mm(l, u) + z for x, l, u, z in zip(ars, low_rb, us, rkv)]
    upd = [_mm(jnp.concatenate([u, v.astype(F32)], axis=0).T, y) for u, v, y in zip(us, v2, bk)]
    for u in units:
        s_ref[u] = (ss[u] + upd[u]) * g_end[u]

    inv_n = 1.0 / RW_HEAD_DIM
    ys = [y[:ch] + y[ch:] for y in y2]
    ds = [y - segsum(y) * inv_n for y in ys]
    var = [segsum(d * d) * inv_n for d in ds]
    for u in units:
        bi, sl = u // groups, sls[u % groups]
        yn = ds[u] * lax.rsqrt(var[u] + GN_EPS) * gnw_ref[:, sl] + gnb_ref[:, sl]
        rkr, v = bonus_in[u]
        g = gate[u]
        o_ref[bi, :, sl] = ((yn + segsum(rkr) * v) * (g * _sigmoid(g))).astype(BF16)


def _wkv(pm, lora, pm_first, w0, a0, k_k, k_a, r_k, v0, gn_w, gn_b, dup, iup, vup, *, batch, seq, width, vres):
    m = pm.shape[0]
    nc = seq // WKV_CHUNK
    tc = min(WKV_GROUPS * LANES, width)
    ncb = width // tc
    nb = math.gcd(WKV_SEQS, batch)
    view = lambda a: a.reshape(batch, seq, a.shape[1])
    col = lambda c0: pl.BlockSpec((nb, WKV_CHUNK, tc), lambda b, j, c: (b, c, c0 * ncb + j))
    vec = pl.BlockSpec((1, tc), lambda b, j, c: (0, j))
    up = pl.BlockSpec((LORA_PAD, tc), lambda b, j, c: (0, j))
    row = lambda a: a.reshape(1, width)
    out = pl.pallas_call(
        functools.partial(_wkv_kernel, vres=vres),
        grid=(batch // nb, ncb, nc),
        in_specs=[col(0), col(1), col(2), col(3),
                  pl.BlockSpec((nb, WKV_CHUNK, lora.shape[1]), lambda b, j, c: (b, c, 0)), col(2),
                  vec, vec, vec, vec, vec, vec, vec, vec, up, up, up],
        out_specs=pl.BlockSpec((nb, WKV_CHUNK, tc), lambda b, j, c: (b, c, j)),
        out_shape=jax.ShapeDtypeStruct((batch, seq, width), BF16),
        scratch_shapes=[pltpu.VMEM((nb * tc // LANES, 2 * WKV_CHUNK, 2 * WKV_CHUNK), F32)],
        compiler_params=_params(("parallel", "parallel", "arbitrary")),
        name="wkv7_chunked",
    )(view(pm), view(pm), view(pm), view(pm), view(lora), view(pm_first), row(w0), row(a0), row(k_k), row(k_a),
      row(r_k), row(v0), row(gn_w), row(gn_b), dup, iup, vup)
    return out.reshape(m, width)


def _tiles(m, seq):
    return dict(tm=min(1024, seq), tn=min(1024, 4 * DA_VALUE_DIM), tr=min(512, seq), tq=min(512, seq))


def _attention_layer(x, cos, sin, norm_w, w_in, q_gain, k_gain, lq1, lk1, lq2, lk2, subln_w, w_out, lambda_init,
                     *, batch, seq, t):
    d = x.shape[1]
    heads = d // DA_VALUE_DIM
    proj = _attn_in_proj(x, norm_w, w_in.astype(BF16), cos, sin, q_gain, k_gain, tm=t["tm"], tn=min(t["tn"], d))
    o = _flash_diff_attention(proj, lq1, lk1, lq2, lk2, subln_w, batch=batch, seq=seq, heads=heads,
                              lambda_init=lambda_init, tq=t["tq"])
    return _matmul_residual(o, w_out.astype(BF16), x, tm=t["tm"], tn=min(t["tn"], d))


def _pad_rows(w, rows):
    return jnp.pad(w, ((0, rows - w.shape[0]), (0, 0)))


def _pad_cols(w, cols):
    return jnp.pad(w, ((0, 0), (0, cols - w.shape[1])))


def _rwkv_layer(x, pm_first, norm_w, w_in, mu, w0, decay_up, a0, iclr_up, k_k, k_a, r_k, gn_w, gn_b, w_out, vres,
                *, batch, seq, t):
    d = x.shape[1]
    wide = 4 * d
    dr = decay_up.shape[0]
    ir = iclr_up.shape[0]
    lora_w = [_pad_cols(w_in[:, wide:wide + dr], LORA_PAD), _pad_cols(w_in[:, wide + dr:wide + dr + ir], LORA_PAD)]
    lora_mu = [jnp.pad(mu[wide:wide + dr], (0, LORA_PAD - dr)), jnp.pad(mu[wide + dr:wide + dr + ir], (0, LORA_PAD - ir))]
    if vres is not None:
        vd_w, vd_mu, v0, vu_w = vres
        lora_w.append(_pad_cols(vd_w, LORA_PAD))
        lora_mu.append(jnp.pad(vd_mu, (0, LORA_PAD - vd_mu.shape[0])))
        vup = _pad_rows(vu_w, LORA_PAD).astype(BF16)
    else:
        v0 = jnp.zeros((d,), F32)
        vup = jnp.zeros((LORA_PAD, d), BF16)
    lora_w = jnp.concatenate(lora_w, axis=1).astype(BF16)
    lora_mu = jnp.concatenate(lora_mu)
    pm, lora = _rwkv_in_proj(x, norm_w, w_in.astype(BF16), mu[:wide], lora_w, lora_mu, rows_per_seq=seq,
                             tm=t["tm"], tn=t["tn"])
    o = _wkv(pm, lora, pm if pm_first is None else pm_first, w0, a0, k_k, k_a, r_k.reshape(-1), v0, gn_w, gn_b,
             _pad_rows(decay_up, LORA_PAD).astype(BF16), _pad_rows(iclr_up, LORA_PAD).astype(BF16), vup,
             batch=batch, seq=seq, width=d, vres=vres is not None)
    return _matmul_residual(o, w_out.astype(BF16), x, tm=t["tm"], tn=min(t["tn"], d)), pm


def kernel(x, positions, norm_w, da_w_in, da_q_gain, da_k_gain, da_lam_q1, da_lam_k1, da_lam_q2, da_lam_k2, da_subln_w, da_w_out, rw_w_in, rw_mu, rw_w0, rw_decay_up, rw_a0, rw_iclr_up, rw_k_k, rw_k_a, rw_r_k, rw_gn_w, rw_gn_b, rw_w_out, rw_vres_down, rw_vres_mu, rw_v0, rw_vres_up):
    batch, seq, d = x.shape
    depth = norm_w.shape[0]
    m = batch * seq
    t = _tiles(m, seq)
    xf = x.reshape(m, d)
    cos, sin = _rope_tables(positions.reshape(m), tr=t["tr"])
    pm_first = None
    for i in range(depth):
        j = i // 2
        if i % 2 == 0:
            lambda_init = 0.8 - 0.6 * math.exp(-0.3 * i)
            xf = _attention_layer(xf, cos, sin, norm_w[i], da_w_in[j], da_q_gain[j], da_k_gain[j], da_lam_q1[j],
                                  da_lam_k1[j], da_lam_q2[j], da_lam_k2[j], da_subln_w[j], da_w_out[j], lambda_init,
                                  batch=batch, seq=seq, t=t)
        else:
            vres = None if j == 0 else (rw_vres_down[j - 1], rw_vres_mu[j - 1], rw_v0[j - 1], rw_vres_up[j - 1])
            xf, pm_now = _rwkv_layer(xf, pm_first, norm_w[i], rw_w_in[j], rw_mu[j], rw_w0[j], rw_decay_up[j], rw_a0[j],
                                     rw_iclr_up[j], rw_k_k[j], rw_k_a[j], rw_r_k[j], rw_gn_w[j], rw_gn_b[j],
                                     rw_w_out[j], vres, batch=batch, seq=seq, t=t)
            if pm_first is None:
                pm_first = pm_now
    return xf.reshape(batch, seq, d)
```

```python
import functools
import math

import jax
import jax.numpy as jnp
from jax import lax
from jax.experimental import pallas as pl
from jax.experimental.pallas import tpu as pltpu

F32 = jnp.float32
BF16 = jnp.bfloat16

LANES = 128
DA_HEAD_DIM = 128
DA_VALUE_DIM = 2 * DA_HEAD_DIM
RW_HEAD_DIM = 64
WKV_CHUNK = 64
FLASH_GRANULES = 4
WKV_GROUPS = 16
WKV_SEQS = 2
LORA_PAD = 128
ROPE_THETA = 10000.0
NORM_EPS = 1e-6
GN_EPS = 64e-5
NEG_BIG = -1e30
VMEM_LIMIT = 56 * 1024 * 1024


def _params(sem):
    return pltpu.CompilerParams(dimension_semantics=sem, vmem_limit_bytes=VMEM_LIMIT)


def _dot(a, b):
    return jnp.dot(a, b, preferred_element_type=F32)


def _dot_nt(a, b):
    return lax.dot_general(a, b, (((1,), (1,)), ((), ())), preferred_element_type=F32)


def _split2(x):
    hi = x.astype(BF16)
    lo = (x - hi.astype(F32)).astype(BF16)
    return hi, lo


def _mm(a, b, nt=False):
    d = _dot_nt if nt else _dot
    return d(a.astype(BF16), b.astype(BF16))


def _sigmoid(x):
    return 1.0 / (1.0 + jnp.exp(-x))


def _store_normed(x_ref, g_ref, xn_ref):
    x = x_ref[...]
    ms = jnp.mean(x * x, axis=-1, keepdims=True)
    xn_ref[...] = (x * lax.rsqrt(ms + NORM_EPS) * g_ref[...]).astype(BF16)


def _rwkv_in_kernel(x_ref, g_ref, w_ref, mu_ref, lw_ref, lmu_ref, o_ref, lo_ref, xn_ref, prev_ref, lprev_ref,
                    *, rows_per_seq):
    i = pl.program_id(0)
    j = pl.program_id(1)
    tm = x_ref.shape[0]
    seq_start = (i * tm) % rows_per_seq == 0

    def shift_mix(acc, prev, mu):
        row = lax.broadcasted_iota(jnp.int32, acc.shape, 0)
        shifted = jnp.where(row == 0, jnp.where(seq_start, 0.0, prev), pltpu.roll(acc, 1, axis=0))
        return acc + (shifted - acc) * mu

    @pl.when(i == 0)
    def _():
        prev_ref[j] = jnp.zeros(prev_ref.shape[1:], F32)

    @pl.when(jnp.logical_and(i == 0, j == 0))
    def _():
        lprev_ref[...] = jnp.zeros(lprev_ref.shape, F32)

    @pl.when(j == 0)
    def _():
        _store_normed(x_ref, g_ref, xn_ref)
        lacc = _dot(xn_ref[...], lw_ref[...])
        lo_ref[...] = shift_mix(lacc, lprev_ref[...], lmu_ref[...])
        lprev_ref[...] = lacc[tm - 1:tm, :]

    acc = _dot(xn_ref[...], w_ref[...])
    o_ref[...] = shift_mix(acc, prev_ref[j], mu_ref[...])
    prev_ref[j] = acc[tm - 1:tm, :]


def _rwkv_in_proj(x, gain, w_bf16, mu, lora_w, lora_mu, *, rows_per_seq, tm, tn):
    m, d = x.shape
    n = mu.shape[0]
    nl = lora_w.shape[1]
    assert m % tm == 0 and n % tn == 0 and rows_per_seq % tm == 0 and n <= w_bf16.shape[1]
    return pl.pallas_call(
        functools.partial(_rwkv_in_kernel, rows_per_seq=rows_per_seq),
        grid=(m // tm, n // tn),
        in_specs=[
            pl.BlockSpec((tm, d), lambda i, j: (i, 0)),
            pl.BlockSpec((1, d), lambda i, j: (0, 0)),
            pl.BlockSpec((d, tn), lambda i, j: (0, j)),
            pl.BlockSpec((1, tn), lambda i, j: (0, j)),
            pl.BlockSpec((d, nl), lambda i, j: (0, 0)),
            pl.BlockSpec((1, nl), lambda i, j: (0, 0)),
        ],
        out_specs=[pl.BlockSpec((tm, tn), lambda i, j: (i, j)), pl.BlockSpec((tm, nl), lambda i, j: (i, 0))],
        out_shape=[jax.ShapeDtypeStruct((m, n), F32), jax.ShapeDtypeStruct((m, nl), F32)],
        scratch_shapes=[pltpu.VMEM((tm, d), BF16), pltpu.VMEM((n // tn, 1, tn), F32), pltpu.VMEM((1, nl), F32)],
        compiler_params=_params(("arbitrary", "arbitrary")),
        name="rwkv_in_proj",
    )(x, gain.reshape(1, d), w_bf16, mu.reshape(1, n), lora_w, lora_mu.reshape(1, nl))


def _attn_in_kernel(x_ref, g_ref, w_ref, cos_ref, sin_ref, qg_ref, kg_ref, o_ref, xn_ref, acc_ref, *, sec, scale):
    j = pl.program_id(1)
    last = pl.num_programs(1) - 1

    def matmul():
        acc_ref[...] = _dot(xn_ref[...], w_ref[...])

    def finish_qk():
        is_q = j - 1 < sec
        gain = jnp.where(is_q, qg_ref[...], kg_ref[...])
        mult = jnp.where(is_q, scale, 1.0)
        cos = cos_ref[...]
        lane = lax.broadcasted_iota(jnp.int32, cos.shape, 1)
        sin_signed = jnp.where(lane < DA_HEAD_DIM // 2, -sin_ref[...], sin_ref[...])
        for c in range(acc_ref.shape[1] // DA_HEAD_DIM):
            sl = slice(c * DA_HEAD_DIM, (c + 1) * DA_HEAD_DIM)
            x = acc_ref[:, sl]
            y = x * lax.rsqrt(jnp.mean(x * x, axis=-1, keepdims=True) + NORM_EPS) * gain
            rot = pltpu.roll(y, DA_HEAD_DIM // 2, axis=1)
            o_ref[:, sl] = ((y * cos + rot * sin_signed) * mult).astype(BF16)

    def finish_plain():
        o_ref[...] = acc_ref[...].astype(BF16)

    @pl.when(j == 0)
    def _():
        _store_normed(x_ref, g_ref, xn_ref)
        matmul()

    @pl.when(jnp.logical_and(j >= 1, j <= 2 * sec))
    def _():
        finish_qk()
        matmul()

    @pl.when(jnp.logical_and(j > 2 * sec, j < last))
    def _():
        finish_plain()
        matmul()

    @pl.when(j == last)
    def _():
        finish_plain()


def _attn_in_proj(x, gain, w_bf16, cos, sin, q_gain, k_gain, *, tm, tn):
    m, d = x.shape
    n = w_bf16.shape[1]
    nj = n // tn
    assert m % tm == 0 and d % tn == 0 and n == 4 * d
    rope = pl.BlockSpec((tm, DA_HEAD_DIM), lambda i, j: (i, 0))
    vec = pl.BlockSpec((1, DA_HEAD_DIM), lambda i, j: (0, 0))
    return pl.pallas_call(
        functools.partial(_attn_in_kernel, sec=d // tn, scale=DA_HEAD_DIM ** -0.5 * math.log2(math.e)),
        grid=(m // tm, nj + 1),
        in_specs=[
            pl.BlockSpec((tm, d), lambda i, j: (i, 0)),
            pl.BlockSpec((1, d), lambda i, j: (0, 0)),
            pl.BlockSpec((d, tn), lambda i, j: (0, jnp.minimum(j, nj - 1))),
            rope, rope, vec, vec,
        ],
        out_specs=pl.BlockSpec((tm, tn), lambda i, j: (i, jnp.maximum(j - 1, 0))),
        out_shape=jax.ShapeDtypeStruct((m, n), BF16),
        scratch_shapes=[pltpu.VMEM((tm, d), BF16), pltpu.VMEM((tm, tn), F32)],
        compiler_params=_params(("parallel", "arbitrary")),
        name="attn_in_proj",
    )(x, gain.reshape(1, d), w_bf16, cos, sin, q_gain.reshape(1, -1), k_gain.reshape(1, -1))


def _mm_res_kernel(a_ref, w_ref, x_ref, o_ref):
    o_ref[...] = x_ref[...] + _dot(a_ref[...], w_ref[...])


def _matmul_residual(a_bf16, w_bf16, x, *, tm, tn):
    m, k = a_bf16.shape
    n = w_bf16.shape[1]
    assert m % tm == 0 and n % tn == 0
    return pl.pallas_call(
        _mm_res_kernel,
        grid=(m // tm, n // tn),
        in_specs=[
            pl.BlockSpec((tm, k), lambda i, j: (i, 0)),
            pl.BlockSpec((k, tn), lambda i, j: (0, j)),
            pl.BlockSpec((tm, tn), lambda i, j: (i, j)),
        ],
        out_specs=pl.BlockSpec((tm, tn), lambda i, j: (i, j)),
        out_shape=jax.ShapeDtypeStruct((m, n), F32),
        compiler_params=_params(("parallel", "arbitrary")),
        name="out_proj_residual",
    )(a_bf16, w_bf16, x)


def _rope_kernel(pos_ref, inv_ref, cos_ref, sin_ref):
    ang = pos_ref[...].astype(F32) * inv_ref[...]
    cos_ref[...] = jnp.cos(ang)
    sin_ref[...] = jnp.sin(ang)


def _rope_tables(positions_flat, *, tr):
    m = positions_flat.shape[0]
    inv = ROPE_THETA ** (-jnp.arange(0, DA_HEAD_DIM, 2, dtype=F32) / DA_HEAD_DIM)
    inv = jnp.concatenate([inv, inv]).reshape(1, DA_HEAD_DIM)
    return pl.pallas_call(
        _rope_kernel,
        grid=(m // tr,),
        in_specs=[pl.BlockSpec((tr, 1), lambda i: (i, 0)), pl.BlockSpec((1, DA_HEAD_DIM), lambda i: (0, 0))],
        out_specs=[pl.BlockSpec((tr, DA_HEAD_DIM), lambda i: (i, 0))] * 2,
        out_shape=[jax.ShapeDtypeStruct((m, DA_HEAD_DIM), F32)] * 2,
        compiler_params=_params(("parallel",)),
        name="rope_tables",
    )(positions_flat.reshape(m, 1), inv)


def _flash_kernel(qtab_ref, ktab_ref, otab_ref, mode_ref, q_ref, k_ref, v_ref, gate_ref, lq1_ref, lk1_ref, lq2_ref,
                  lk2_ref, sub_ref, o_ref, m_ref, l_ref, acc_ref, pl_ref, pacc_ref, *, lambda_init, modes):
    step_id = pl.program_id(2)
    kb = ktab_ref[step_id]
    mode = mode_ref[step_id]
    tq = q_ref.shape[0]
    per_block = k_ref.shape[0] // tq
    comps = range(2)
    sls = [slice(c * DA_HEAD_DIM, (c + 1) * DA_HEAD_DIM) for c in comps]

    @pl.when(kb == 0)
    def _():
        m_ref[...] = jnp.full(m_ref.shape, NEG_BIG, F32)
        l_ref[...] = jnp.zeros(l_ref.shape, F32)
        acc_ref[...] = jnp.zeros(acc_ref.shape, F32)

    def process(granules, diag):
        below = (granules - 1) * tq if diag else granules * tq
        spans = ([(0, below, False)] if below else []) + ([(below, below + tq, True)] if diag else [])
        s = [[_dot_nt(q_ref[:, sl], k_ref[lo:hi, sl]) for lo, hi, _ in spans] for sl in sls]
        if diag:
            row = lax.broadcasted_iota(jnp.int32, (tq, tq), 0)
            col = lax.broadcasted_iota(jnp.int32, (tq, tq), 1)
            keep = col <= row
            s = [xs[:-1] + [jnp.where(keep, xs[-1], NEG_BIG)] for xs in s]
        m = []
        for c in comps:
            m_new = m_ref[c]
            for x in s[c]:
                m_new = jnp.maximum(m_new, jnp.max(x, axis=-1, keepdims=True))
            m.append(m_new)
        alpha = [jnp.exp2(m_ref[c] - m[c]) for c in comps]
        p = [[jnp.exp2(x - jnp.concatenate([m[c]] * ((hi - lo) // LANES), axis=1))
              for x, (lo, hi, _) in zip(s[c], spans)] for c in comps]
        pv = [[_dot(x.astype(BF16), v_ref[lo:hi, :]) for x, (lo, hi, _) in zip(p[c], spans)] for c in comps]
        l, acc = [], []
        for c in comps:
            part = alpha[c] * l_ref[c]
            for x in p[c]:
                for j in range(x.shape[1] // LANES):
                    part = part + x[:, j * LANES:(j + 1) * LANES]
            l.append(part)
            acc.append(jnp.concatenate([alpha[c]] * (DA_VALUE_DIM // LANES), axis=1) * acc_ref[c] + sum(pv[c]))
        return m, l, acc

    def carry(m, l, acc):
        for c in comps:
            m_ref[c] = m[c]
            l_ref[c] = l[c]
            acc_ref[c] = acc[c]

    def defer(l, acc):
        for c in comps:
            pl_ref[c] = l[c]
            pacc_ref[c] = acc[c]

    def finish():
        l = [pl_ref[c] for c in comps]
        acc = [pacc_ref[c] for c in comps]
        lam = (jnp.exp(jnp.sum(lq1_ref[...] * lk1_ref[...], axis=-1, keepdims=True))
               - jnp.exp(jnp.sum(lq2_ref[...] * lk2_ref[...], axis=-1, keepdims=True)) + lambda_init)
        l0 = jnp.sum(l[0], axis=-1, keepdims=True)
        l1 = jnp.sum(l[1], axis=-1, keepdims=True)
        o = acc[0] * (1.0 / l0) - acc[1] * (lam / l1)
        o = o * lax.rsqrt(jnp.mean(o * o, axis=-1, keepdims=True) + NORM_EPS) * sub_ref[...]
        g = gate_ref[...].astype(F32)
        o_ref[...] = (o * (1.0 - lambda_init) * (g * _sigmoid(g))).astype(BF16)

    for code, (kind, pending) in enumerate(modes):
        @pl.when(mode == code)
        def _(kind=kind, pending=pending):
            if pending:
                finish()
            if kind == 0:
                carry(*process(per_block, False))
            elif kind > 0:
                _, l, acc = process(kind, True)
                defer(l, acc)


def _flash_diff_attention(proj, lam_q1, lam_k1, lam_q2, lam_k2, subln_w, *, batch, seq, heads,
                          lambda_init, tq):
    m = proj.shape[0]
    nq = seq // tq
    per = min(FLASH_GRANULES, nq)
    assert nq % per == 0
    steps = [(a, b, 0 if b < a // per else 1 + a % per) for a in range(nq) for b in range(a // per + 1)]
    steps.append((steps[-1][0], steps[-1][1], -1))
    pending = [False] + [kind > 0 for _, _, kind in steps[:-1]]
    modes = sorted(set(zip((kind for _, _, kind in steps), pending)))
    qtab = jnp.asarray([a for a, _, _ in steps], jnp.int32)
    ktab = jnp.asarray([b for _, b, _ in steps], jnp.int32)
    otab = jnp.asarray([steps[max(i - 1, 0)][0] for i in range(len(steps))], jnp.int32)
    mode = jnp.asarray([modes.index((kind, p)) for (_, _, kind), p in zip(steps, pending)], jnp.int32)
    blk = lambda rows, tab, c0: pl.BlockSpec(
        (rows, DA_VALUE_DIM), lambda b, h, s, qt, kt, ot, md: (b * (seq // rows) + (qt, kt, ot)[tab][s], c0 * heads + h))
    vec = lambda n: pl.BlockSpec((1, n), lambda b, h, s, qt, kt, ot, md: (0, 0))
    kern = functools.partial(_flash_kernel, lambda_init=lambda_init, modes=modes)
    grid_spec = pltpu.PrefetchScalarGridSpec(
        num_scalar_prefetch=4,
        grid=(batch, heads, len(steps)),
        in_specs=[
            blk(tq, 0, 0), blk(per * tq, 1, 1), blk(per * tq, 1, 2), blk(tq, 2, 3),
            vec(DA_HEAD_DIM), vec(DA_HEAD_DIM), vec(DA_HEAD_DIM), vec(DA_HEAD_DIM), vec(DA_VALUE_DIM),
        ],
        out_specs=blk(tq, 2, 0),
        scratch_shapes=[pltpu.VMEM((2, tq, LANES), F32), pltpu.VMEM((2, tq, LANES), F32),
                        pltpu.VMEM((2, tq, DA_VALUE_DIM), F32),
                        pltpu.VMEM((2, tq, LANES), F32), pltpu.VMEM((2, tq, DA_VALUE_DIM), F32)],
    )
    return pl.pallas_call(
        kern,
        grid_spec=grid_spec,
        out_shape=jax.ShapeDtypeStruct((m, heads * DA_VALUE_DIM), BF16),
        compiler_params=_params(("parallel", "parallel", "arbitrary")),
        name="flash_diff_attention",
    )(qtab, ktab, otab, mode, proj, proj, proj, proj, lam_q1.reshape(1, -1), lam_k1.reshape(1, -1),
      lam_q2.reshape(1, -1), lam_k2.reshape(1, -1), subln_w.reshape(1, -1))


def _unit_lower_inverse(lows, row, col):
    blk = lambda sh: (row >> sh) == (col >> sh)
    eye = jnp.where(row == col, 1.0, 0.0)
    d1 = [jnp.where(blk(3), lo, 0.0) for lo in lows]
    d1b = [d.astype(BF16) for d in d1]
    d2 = [_mm(d, d) for d in d1b]
    d2b = [d.astype(BF16) for d in d2]
    d4 = [_mm(d, d) for d in d2b]
    xs = [eye + a + b + _mm(ab, bb) for a, b, ab, bb in zip(d1, d2, d1b, d2b)]
    xs = [x + _mm(x, d) for x, d in zip(xs, d4)]
    for sh in (3, 4, 5):
        off = jnp.logical_and(blk(sh + 1), jnp.logical_not(blk(sh)))
        ns = [jnp.where(off, lo, 0.0) for lo in lows]
        xb = [x.astype(BF16) for x in xs]
        ts = [_mm(x, n) for x, n in zip(xb, ns)]
        xs = [x + _mm(t, y) for x, t, y in zip(xs, ts, xb)]
    return xs


def _wkv_kernel(r_ref, k_ref, v_ref, g_ref, lora_ref, vfirst_ref, w0_ref, a0_ref, kk_ref, ka_ref, rk_ref, v0_ref,
                gnw_ref, gnb_ref, dup_ref, iup_ref, vup_ref, o_ref, s_ref, *, vres):
    c = pl.program_id(2)
    seqs = range(r_ref.shape[0])
    groups = r_ref.shape[2] // LANES
    ch = WKV_CHUNK
    n2 = 2 * ch

    @pl.when(c == 0)
    def _():
        s_ref[...] = jnp.zeros(s_ref.shape, F32)

    row = lax.broadcasted_iota(jnp.int32, (n2, n2), 0)
    col = lax.broadcasted_iota(jnp.int32, (n2, n2), 1)
    strict = col < row
    incl = col <= row
    lane = lax.broadcasted_iota(jnp.int32, (ch, LANES), 1)
    first = lane < RW_HEAD_DIM
    m0 = jnp.where(first, 1.0, 0.0)
    m1 = 1.0 - m0
    r64 = lax.broadcasted_iota(jnp.int32, (ch, ch), 0)
    c64 = lax.broadcasted_iota(jnp.int32, (ch, ch), 1)
    tri = jnp.where(c64 <= r64, 1.0, 0.0).astype(BF16)
    sls = [slice(u * LANES, (u + 1) * LANES) for u in range(groups)]

    def segsum(x):
        s0 = jnp.sum(x * m0, axis=-1, keepdims=True)
        s1 = jnp.sum(x * m1, axis=-1, keepdims=True)
        return jnp.where(first, s0, s1)

    def stack2(x):
        return jnp.concatenate([jnp.where(first, x, 0), jnp.where(first, 0, x)], axis=0)

    a2, b2, k2, r2, v2, g_end, bonus_in, gate = [], [], [], [], [], [], [], []
    for bi in seqs:
        lora = lora_ref[bi]
        dwt = jnp.tanh(lora[:, 0:LORA_PAD]).astype(BF16)
        da = lora[:, LORA_PAD:2 * LORA_PAD].astype(BF16)
        lw = -math.exp(-0.5) * _sigmoid(w0_ref[...] + _dot(dwt, dup_ref[...]))
        a = _sigmoid(a0_ref[...] + _dot(da, iup_ref[...]))
        r = r_ref[bi]
        k = k_ref[bi]
        v = v_ref[bi]
        if vres:
            pv = lora[:, 2 * LORA_PAD:3 * LORA_PAD].astype(BF16)
            v = v + (vfirst_ref[bi] - v) * _sigmoid(v0_ref[...] + _dot(pv, vup_ref[...]))
        kk = k * kk_ref[...]
        k_mod = k * (1.0 + (a - 1.0) * ka_ref[...])
        rkr = r * k_mod * rk_ref[...]
        lw_hi, lw_lo = _split2(lw)
        cum = _dot(tri, lw_hi) + _dot(tri, lw_lo)
        g_in = jnp.exp(cum)
        g_inv = jnp.exp(-cum)
        g_ex = jnp.exp(cum - lw)
        rt = (r * g_in).astype(BF16)
        kt = (k_mod * g_inv).astype(BF16)
        vb = v.astype(BF16)
        for sl in sls:
            kc = kk[:, sl]
            kn = kc * lax.rsqrt(jnp.maximum(segsum(kc * kc), 1e-24))
            a2.append(stack2((-kn * g_ex[:, sl]).astype(BF16)))
            b2.append(stack2((kn * a[:, sl] * g_inv[:, sl]).astype(BF16)))
            k2.append(stack2(kt[:, sl]))
            r2.append(stack2(rt[:, sl]))
            v2.append(stack2(vb[:, sl]))
            g_end.append(g_in[ch - 1:ch, sl])
            bonus_in.append((rkr[:, sl], v[:, sl]))
            gate.append(g_ref[bi, :, sl])
    units = range(len(a2))

    bk = [jnp.concatenate([b, k], axis=0) for b, k in zip(b2, k2)]
    sc = [_mm(jnp.concatenate([a, r], axis=0), x, nt=True) for a, r, x in zip(a2, r2, bk)]
    low_ab = [jnp.where(strict, s[:n2, :n2], 0.0) for s in sc]
    low_ak = [jnp.where(strict, s[:n2, n2:], 0.0) for s in sc]
    low_rb = [jnp.where(incl, s[n2:, :n2], 0.0) for s in sc]
    low_rk = [jnp.where(incl, s[n2:, n2:], 0.0) for s in sc]

    w1 = [_mm(l, v) for l, v in zip(low_ak, v2)]
    rkv = [_mm(l, v) for l, v in zip(low_rk, v2)]
    tinv = _unit_lower_inverse(low_ab, row, col)
    au = [_mm(t, jnp.concatenate([a, w.astype(BF16)], axis=1)) for t, a, w in zip(tinv, a2, w1)]

    ss = [s_ref[u] for u in units]
    ars = [_mm(jnp.concatenate([x[:, :n2].astype(BF16), r], axis=0), s, nt=True) for x, r, s in zip(au, r2, ss)]
    us = [x[:n2] + y[:, n2:] for x, y in zip(ars, au)]
    y2 = [x[n2:] + _mm(l, u) + z for x, l, u, z in zip(ars, low_rb, us, rkv)]
    upd = [_mm(jnp.concatenate([u, v.astype(F32)], axis=0).T, y) for u, v, y in zip(us, v2, bk)]
    for u in units:
        s_ref[u] = (ss[u] + upd[u]) * g_end[u]

    inv_n = 1.0 / RW_HEAD_DIM
    ys = [y[:ch] + y[ch:] for y in y2]
    ds = [y - segsum(y) * inv_n for y in ys]
    var = [segsum(d * d) * inv_n for d in ds]
    for u in units:
        bi, sl = u // groups, sls[u % groups]
        yn = ds[u] * lax.rsqrt(var[u] + GN_EPS) * gnw_ref[:, sl] + gnb_ref[:, sl]
        rkr, v = bonus_in[u]
        g = gate[u]
        o_ref[bi, :, sl] = ((yn + segsum(rkr) * v) * (g * _sigmoid(g))).astype(BF16)


def _wkv(pm, lora, pm_first, w0, a0, k_k, k_a, r_k, v0, gn_w, gn_b, dup, iup, vup, *, batch, seq, width, vres):
    m = pm.shape[0]
    nc = seq // WKV_CHUNK
    tc = min(WKV_GROUPS * LANES, width)
    ncb = width // tc
    nb = math.gcd(WKV_SEQS, batch)
    view = lambda a: a.reshape(batch, seq, a.shape[1])
    col = lambda c0: pl.BlockSpec((nb, WKV_CHUNK, tc), lambda b, j, c: (b, c, c0 * ncb + j))
    vec = pl.BlockSpec((1, tc), lambda b, j, c: (0, j))
    up = pl.BlockSpec((LORA_PAD, tc), lambda b, j, c: (0, j))
    row = lambda a: a.reshape(1, width)
    out = pl.pallas_call(
        functools.partial(_wkv_kernel, vres=vres),
        grid=(batch // nb, ncb, nc),
        in_specs=[col(0), col(1), col(2), col(3),
                  pl.BlockSpec((nb, WKV_CHUNK, lora.shape[1]), lambda b, j, c: (b, c, 0)), col(2),
                  vec, vec, vec, vec, vec, vec, vec, vec, up, up, up],
        out_specs=pl.BlockSpec((nb, WKV_CHUNK, tc), lambda b, j, c: (b, c, j)),
        out_shape=jax.ShapeDtypeStruct((batch, seq, width), BF16),
        scratch_shapes=[pltpu.VMEM((nb * tc // LANES, 2 * WKV_CHUNK, 2 * WKV_CHUNK), F32)],
        compiler_params=_params(("parallel", "parallel", "arbitrary")),
        name="wkv7_chunked",
    )(view(pm), view(pm), view(pm), view(pm), view(lora), view(pm_first), row(w0), row(a0), row(k_k), row(k_a),
      row(r_k), row(v0), row(gn_w), row(gn_b), dup, iup, vup)
    return out.reshape(m, width)


def _tiles(m, seq):
    return dict(tm=min(1024, seq), tn=min(1024, 4 * DA_VALUE_DIM), tr=min(512, seq), tq=min(512, seq))


def _attention_layer(x, cos, sin, norm_w, w_in, q_gain, k_gain, lq1, lk1, lq2, lk2, subln_w, w_out, lambda_init,
                     *, batch, seq, t):
    d = x.shape[1]
    heads = d // DA_VALUE_DIM
    proj = _attn_in_proj(x, norm_w, w_in.astype(BF16), cos, sin, q_gain, k_gain, tm=t["tm"], tn=min(t["tn"], d))
    o = _flash_diff_attention(proj, lq1, lk1, lq2, lk2, subln_w, batch=batch, seq=seq, heads=heads,
                              lambda_init=lambda_init, tq=t["tq"])
    return _matmul_residual(o, w_out.astype(BF16), x, tm=t["tm"], tn=min(t["tn"], d))


def _pad_rows(w, rows):
    return jnp.pad(w, ((0, rows - w.shape[0]), (0, 0)))


def _pad_cols(w, cols):
    return jnp.pad(w, ((0, 0), (0, cols - w.shape[1])))


def _rwkv_layer(x, pm_first, norm_w, w_in, mu, w0, decay_up, a0, iclr_up, k_k, k_a, r_k, gn_w, gn_b, w_out, vres,
                *, batch, seq, t):
    d = x.shape[1]
    wide = 4 * d
    dr = decay_up.shape[0]
    ir = iclr_up.shape[0]
    lora_w = [_pad_cols(w_in[:, wide:wide + dr], LORA_PAD), _pad_cols(w_in[:, wide + dr:wide + dr + ir], LORA_PAD)]
    lora_mu = [jnp.pad(mu[wide:wide + dr], (0, LORA_PAD - dr)), jnp.pad(mu[wide + dr:wide + dr + ir], (0, LORA_PAD - ir))]
    if vres is not None:
        vd_w, vd_mu, v0, vu_w = vres
        lora_w.append(_pad_cols(vd_w, LORA_PAD))
        lora_mu.append(jnp.pad(vd_mu, (0, LORA_PAD - vd_mu.shape[0])))
        vup = _pad_rows(vu_w, LORA_PAD).astype(BF16)
    else:
        v0 = jnp.zeros((d,), F32)
        vup = jnp.zeros((LORA_PAD, d), BF16)
    lora_w = jnp.concatenate(lora_w, axis=1).astype(BF16)
    lora_mu = jnp.concatenate(lora_mu)
    pm, lora = _rwkv_in_proj(x, norm_w, w_in.astype(BF16), mu[:wide], lora_w, lora_mu, rows_per_seq=seq,
                             tm=t["tm"], tn=t["tn"])
    o = _wkv(pm, lora, pm if pm_first is None else pm_first, w0, a0, k_k, k_a, r_k.reshape(-1), v0, gn_w, gn_b,
             _pad_rows(decay_up, LORA_PAD).astype(BF16), _pad_rows(iclr_up, LORA_PAD).astype(BF16), vup,
             batch=batch, seq=seq, width=d, vres=vres is not None)
    return _matmul_residual(o, w_out.astype(BF16), x, tm=t["tm"], tn=min(t["tn"], d)), pm


def kernel(x, positions, norm_w, da_w_in, da_q_gain, da_k_gain, da_lam_q1, da_lam_k1, da_lam_q2, da_lam_k2, da_subln_w, da_w_out, rw_w_in, rw_mu, rw_w0, rw_decay_up, rw_a0, rw_iclr_up, rw_k_k, rw_k_a, rw_r_k, rw_gn_w, rw_gn_b, rw_w_out, rw_vres_down, rw_vres_mu, rw_v0, rw_vres_up):
    batch, seq, d = x.shape
    depth = norm_w.shape[0]
    m = batch * seq
    t = _tiles(m, seq)
    xf = x.reshape(m, d)
    cos, sin = _rope_tables(positions.reshape(m), tr=t["tr"])
    pm_first = None
    for i in range(depth):
        j = i // 2
        if i % 2 == 0:
            lambda_init = 0.8 - 0.6 * math.exp(-0.3 * i)
            xf = _attention_layer(xf, cos, sin, norm_w[i], da_w_in[j], da_q_gain[j], da_k_gain[j], da_lam_q1[j],
                                  da_lam_k1[j], da_lam_q2[j], da_lam_k2[j], da_subln_w[j], da_w_out[j], lambda_init,
                                  batch=batch, seq=seq, t=t)
        else:
            vres = None if j == 0 else (rw_vres_down[j - 1], rw_vres_mu[j - 1], rw_v0[j - 1], rw_vres_up[j - 1])
            xf, pm_now = _rwkv_layer(xf, pm_first, norm_w[i], rw_w_in[j], rw_mu[j], rw_w0[j], rw_decay_up[j], rw_a0[j],
                                     rw_iclr_up[j], rw_k_k[j], rw_k_a[j], rw_r_k[j], rw_gn_w[j], rw_gn_b[j],
                                     rw_w_out[j], vres, batch=batch, seq=seq, t=t)
            if pm_first is None:
                pm_first = pm_now
    return xf.reshape(batch, seq, d)
```

```python
import functools
import math

import jax
import jax.numpy as jnp
from jax import lax
from jax.experimental import pallas as pl
from jax.experimental.pallas import tpu as pltpu

F32 = jnp.float32
BF16 = jnp.bfloat16

LANES = 128
DA_HEAD_DIM = 128
DA_VALUE_DIM = 2 * DA_HEAD_DIM
RW_HEAD_DIM = 64
WKV_CHUNK = 64
FLASH_GRANULES = 4
WKV_GROUPS = 16
WKV_SEQS = 2
LORA_PAD = 128
ROPE_THETA = 10000.0
NORM_EPS = 1e-6
GN_EPS = 64e-5
NEG_BIG = -1e30
VMEM_LIMIT = 56 * 1024 * 1024


def _params(sem):
    return pltpu.CompilerParams(dimension_semantics=sem, vmem_limit_bytes=VMEM_LIMIT)


def _dot(a, b):
    return jnp.dot(a, b, preferred_element_type=F32)


def _dot_nt(a, b):
    return lax.dot_general(a, b, (((1,), (1,)), ((), ())), preferred_element_type=F32)


def _split2(x):
    hi = x.astype(BF16)
    lo = (x - hi.astype(F32)).astype(BF16)
    return hi, lo


def _mm(a, b, nt=False):
    d = _dot_nt if nt else _dot
    return d(a.astype(BF16), b.astype(BF16))


def _sigmoid(x):
    return 1.0 / (1.0 + jnp.exp(-x))


def _store_normed(x_ref, g_ref, xn_ref):
    x = x_ref[...]
    ms = jnp.mean(x * x, axis=-1, keepdims=True)
    xn_ref[...] = (x * lax.rsqrt(ms + NORM_EPS) * g_ref[...]).astype(BF16)


def _rwkv_in_kernel(x_ref, g_ref, w_ref, mu_ref, lw_ref, lmu_ref, o_ref, lo_ref, xn_ref, prev_ref, lprev_ref,
                    *, rows_per_seq):
    i = pl.program_id(0)
    j = pl.program_id(1)
    tm = x_ref.shape[0]
    seq_start = (i * tm) % rows_per_seq == 0

    def shift_mix(acc, prev, mu):
        row = lax.broadcasted_iota(jnp.int32, acc.shape, 0)
        shifted = jnp.where(row == 0, jnp.where(seq_start, 0.0, prev), pltpu.roll(acc, 1, axis=0))
        return acc + (shifted - acc) * mu

    @pl.when(i == 0)
    def _():
        prev_ref[j] = jnp.zeros(prev_ref.shape[1:], F32)

    @pl.when(jnp.logical_and(i == 0, j == 0))
    def _():
        lprev_ref[...] = jnp.zeros(lprev_ref.shape, F32)

    @pl.when(j == 0)
    def _():
        _store_normed(x_ref, g_ref, xn_ref)
        lacc = _dot(xn_ref[...], lw_ref[...])
        lo_ref[...] = shift_mix(lacc, lprev_ref[...], lmu_ref[...])
        lprev_ref[...] = lacc[tm - 1:tm, :]

    acc = _dot(xn_ref[...], w_ref[...])
    o_ref[...] = shift_mix(acc, prev_ref[j], mu_ref[...])
    prev_ref[j] = acc[tm - 1:tm, :]


def _rwkv_in_proj(x, gain, w_bf16, mu, lora_w, lora_mu, *, rows_per_seq, tm, tn):
    m, d = x.shape
    n = mu.shape[0]
    nl = lora_w.shape[1]
    assert m % tm == 0 and n % tn == 0 and rows_per_seq % tm == 0 and n <= w_bf16.shape[1]
    return pl.pallas_call(
        functools.partial(_rwkv_in_kernel, rows_per_seq=rows_per_seq),
        grid=(m // tm, n // tn),
        in_specs=[
            pl.BlockSpec((tm, d), lambda i, j: (i, 0)),
            pl.BlockSpec((1, d), lambda i, j: (0, 0)),
            pl.BlockSpec((d, tn), lambda i, j: (0, j)),
            pl.BlockSpec((1, tn), lambda i, j: (0, j)),
            pl.BlockSpec((d, nl), lambda i, j: (0, 0)),
            pl.BlockSpec((1, nl), lambda i, j: (0, 0)),
        ],
        out_specs=[pl.BlockSpec((tm, tn), lambda i, j: (i, j)), pl.BlockSpec((tm, nl), lambda i, j: (i, 0))],
        out_shape=[jax.ShapeDtypeStruct((m, n), F32), jax.ShapeDtypeStruct((m, nl), F32)],
        scratch_shapes=[pltpu.VMEM((tm, d), BF16), pltpu.VMEM((n // tn, 1, tn), F32), pltpu.VMEM((1, nl), F32)],
        compiler_params=_params(("arbitrary", "arbitrary")),
        name="rwkv_in_proj",
    )(x, gain.reshape(1, d), w_bf16, mu.reshape(1, n), lora_w, lora_mu.reshape(1, nl))


def _attn_in_kernel(x_ref, g_ref, w_ref, cos_ref, sin_ref, qg_ref, kg_ref, o_ref, xn_ref, acc_ref, *, sec, scale):
    j = pl.program_id(1)
    last = pl.num_programs(1) - 1

    def matmul():
        acc_ref[...] = _dot(xn_ref[...], w_ref[...])

    def finish_qk():
        is_q = j - 1 < sec
        gain = jnp.where(is_q, qg_ref[...], kg_ref[...])
        mult = jnp.where(is_q, scale, 1.0)
        cos = cos_ref[...]
        lane = lax.broadcasted_iota(jnp.int32, cos.shape, 1)
        sin_signed = jnp.where(lane < DA_HEAD_DIM // 2, -sin_ref[...], sin_ref[...])
        for c in range(acc_ref.shape[1] // DA_HEAD_DIM):
            sl = slice(c * DA_HEAD_DIM, (c + 1) * DA_HEAD_DIM)
            x = acc_ref[:, sl]
            y = x * lax.rsqrt(jnp.mean(x * x, axis=-1, keepdims=True) + NORM_EPS) * gain
            rot = pltpu.roll(y, DA_HEAD_DIM // 2, axis=1)
            o_ref[:, sl] = ((y * cos + rot * sin_signed) * mult).astype(BF16)

    def finish_plain():
        o_ref[...] = acc_ref[...].astype(BF16)

    @pl.when(j == 0)
    def _():
        _store_normed(x_ref, g_ref, xn_ref)
        matmul()

    @pl.when(jnp.logical_and(j >= 1, j <= 2 * sec))
    def _():
        finish_qk()
        matmul()

    @pl.when(jnp.logical_and(j > 2 * sec, j < last))
    def _():
        finish_plain()
        matmul()

    @pl.when(j == last)
    def _():
        finish_plain()


def _attn_in_proj(x, gain, w_bf16, cos, sin, q_gain, k_gain, *, tm, tn):
    m, d = x.shape
    n = w_bf16.shape[1]
    nj = n // tn
    assert m % tm == 0 and d % tn == 0 and n == 4 * d
    rope = pl.BlockSpec((tm, DA_HEAD_DIM), lambda i, j: (i, 0))
    vec = pl.BlockSpec((1, DA_HEAD_DIM), lambda i, j: (0, 0))
    return pl.pallas_call(
        functools.partial(_attn_in_kernel, sec=d // tn, scale=DA_HEAD_DIM ** -0.5 * math.log2(math.e)),
        grid=(m // tm, nj + 1),
        in_specs=[
            pl.BlockSpec((tm, d), lambda i, j: (i, 0)),
            pl.BlockSpec((1, d), lambda i, j: (0, 0)),
            pl.BlockSpec((d, tn), lambda i, j: (0, jnp.minimum(j, nj - 1))),
            rope, rope, vec, vec,
        ],
        out_specs=pl.BlockSpec((tm, tn), lambda i, j: (i, jnp.maximum(j - 1, 0))),
        out_shape=jax.ShapeDtypeStruct((m, n), BF16),
        scratch_shapes=[pltpu.VMEM((tm, d), BF16), pltpu.VMEM((tm, tn), F32)],
        compiler_params=_params(("parallel", "arbitrary")),
        name="attn_in_proj",
    )(x, gain.reshape(1, d), w_bf16, cos, sin, q_gain.reshape(1, -1), k_gain.reshape(1, -1))


def _mm_res_kernel(a_ref, w_ref, x_ref, o_ref):
    o_ref[...] = x_ref[...] + _dot(a_ref[...], w_ref[...])


def _matmul_residual(a_bf16, w_bf16, x, *, tm, tn):
    m, k = a_bf16.shape
    n = w_bf16.shape[1]
    assert m % tm == 0 and n % tn == 0
    return pl.pallas_call(
        _mm_res_kernel,
        grid=(m // tm, n // tn),
        in_specs=[
            pl.BlockSpec((tm, k), lambda i, j: (i, 0)),
            pl.BlockSpec((k, tn), lambda i, j: (0, j)),
            pl.BlockSpec((tm, tn), lambda i, j: (i, j)),
        ],
        out_specs=pl.BlockSpec((tm, tn), lambda i, j: (i, j)),
        out_shape=jax.ShapeDtypeStruct((m, n), F32),
        compiler_params=_params(("parallel", "arbitrary")),
        name="out_proj_residual",
    )(a_bf16, w_bf16, x)


def _rope_kernel(pos_ref, inv_ref, cos_ref, sin_ref):
    ang = pos_ref[...].astype(F32) * inv_ref[...]
    cos_ref[...] = jnp.cos(ang)
    sin_ref[...] = jnp.sin(ang)


def _rope_tables(positions_flat, *, tr):
    m = positions_flat.shape[0]
    inv = ROPE_THETA ** (-jnp.arange(0, DA_HEAD_DIM, 2, dtype=F32) / DA_HEAD_DIM)
    inv = jnp.concatenate([inv, inv]).reshape(1, DA_HEAD_DIM)
    return pl.pallas_call(
        _rope_kernel,
        grid=(m // tr,),
        in_specs=[pl.BlockSpec((tr, 1), lambda i: (i, 0)), pl.BlockSpec((1, DA_HEAD_DIM), lambda i: (0, 0))],
        out_specs=[pl.BlockSpec((tr, DA_HEAD_DIM), lambda i: (i, 0))] * 2,
        out_shape=[jax.ShapeDtypeStruct((m, DA_HEAD_DIM), F32)] * 2,
        compiler_params=_params(("parallel",)),
        name="rope_tables",
    )(positions_flat.reshape(m, 1), inv)


def _flash_kernel(qtab_ref, ktab_ref, kind_ref, q_ref, k_ref, v_ref, gate_ref, lq1_ref, lk1_ref, lq2_ref, lk2_ref,
                  sub_ref, o_ref, m_ref, l_ref, acc_ref, *, lambda_init, modes):
    step_id = pl.program_id(2)
    mode = kind_ref[step_id]
    tq = q_ref.shape[0]
    per_block = k_ref.shape[0] // tq
    comps = range(2)
    sls = [slice(c * DA_HEAD_DIM, (c + 1) * DA_HEAD_DIM) for c in comps]

    def process(granules, diag, first):
        below = (granules - 1) * tq if diag else granules * tq
        spans = ([(0, below, False)] if below else []) + ([(below, below + tq, True)] if diag else [])
        s = [[_dot_nt(q_ref[:, sl], k_ref[lo:hi, sl]) for lo, hi, _ in spans] for sl in sls]
        if diag:
            row = lax.broadcasted_iota(jnp.int32, (tq, tq), 0)
            col = lax.broadcasted_iota(jnp.int32, (tq, tq), 1)
            keep = col <= row
            s = [xs[:-1] + [jnp.where(keep, xs[-1], NEG_BIG)] for xs in s]
        m = []
        for c in comps:
            m_new = jnp.full((tq, LANES), NEG_BIG, F32) if first else m_ref[c]
            for x in s[c]:
                m_new = jnp.maximum(m_new, jnp.max(x, axis=-1, keepdims=True))
            m.append(m_new)
        if not first:
            alpha = [jnp.exp2(m_ref[c] - m[c]) for c in comps]
        p = [[jnp.exp2(x - jnp.concatenate([m[c]] * ((hi - lo) // LANES), axis=1))
              for x, (lo, hi, _) in zip(s[c], spans)] for c in comps]
        pv = [[_dot(x.astype(BF16), v_ref[lo:hi, :]) for x, (lo, hi, _) in zip(p[c], spans)] for c in comps]
        l, acc = [], []
        for c in comps:
            parts = [x[:, j * LANES:(j + 1) * LANES] for x in p[c] for j in range(x.shape[1] // LANES)]
            if first:
                l.append(sum(parts[1:], parts[0]))
                acc.append(sum(pv[c][1:], pv[c][0]))
            else:
                l.append(sum(parts, alpha[c] * l_ref[c]))
                acc.append(sum(pv[c], jnp.concatenate([alpha[c]] * (DA_VALUE_DIM // LANES), axis=1) * acc_ref[c]))
        return m, l, acc

    def carry(m, l, acc):
        for c in comps:
            m_ref[c] = m[c]
            l_ref[c] = l[c]
            acc_ref[c] = acc[c]

    def finish(l, acc):
        lam = (jnp.exp(jnp.sum(lq1_ref[...] * lk1_ref[...], axis=-1, keepdims=True))
               - jnp.exp(jnp.sum(lq2_ref[...] * lk2_ref[...], axis=-1, keepdims=True)) + lambda_init)
        l0 = jnp.sum(l[0], axis=-1, keepdims=True)
        l1 = jnp.sum(l[1], axis=-1, keepdims=True)
        o = acc[0] / l0 - lam * (acc[1] / l1)
        o = o * lax.rsqrt(jnp.mean(o * o, axis=-1, keepdims=True) + NORM_EPS) * sub_ref[...]
        g = gate_ref[...].astype(F32)
        o_ref[...] = (o * (1.0 - lambda_init) * (g * _sigmoid(g))).astype(BF16)

    for code, (kind, first) in enumerate(modes):
        @pl.when(mode == code)
        def _(kind=kind, first=first):
            if kind == 0:
                carry(*process(per_block, False, first))
            else:
                _, l, acc = process(kind, True, first)
                finish(l, acc)


def _flash_diff_attention(proj, lam_q1, lam_k1, lam_q2, lam_k2, subln_w, *, batch, seq, heads,
                          lambda_init, tq):
    m = proj.shape[0]
    nq = seq // tq
    per = min(FLASH_GRANULES, nq)
    nkb = nq // per
    assert nq % per == 0
    steps = [(a, b, 0 if b < a // per else 1 + a % per) for a in range(nq) for b in range(a // per + 1)]
    modes = sorted({(kind, b == 0) for _, b, kind in steps})
    qtab = jnp.asarray([a for a, _, _ in steps], jnp.int32)
    ktab = jnp.asarray([b for _, b, _ in steps], jnp.int32)
    kinds = jnp.asarray([modes.index((kind, b == 0)) for _, b, kind in steps], jnp.int32)
    q_blk = lambda c0: pl.BlockSpec((tq, DA_VALUE_DIM),
                                    lambda b, h, s, qt, kt, kd: (b * nq + qt[s], c0 * heads + h))
    k_blk = lambda c0: pl.BlockSpec((per * tq, DA_VALUE_DIM),
                                    lambda b, h, s, qt, kt, kd: (b * nkb + kt[s], c0 * heads + h))
    vec = lambda n: pl.BlockSpec((1, n), lambda b, h, s, qt, kt, kd: (0, 0))
    kern = functools.partial(_flash_kernel, lambda_init=lambda_init, modes=modes)
    grid_spec = pltpu.PrefetchScalarGridSpec(
        num_scalar_prefetch=3,
        grid=(batch, heads, len(steps)),
        in_specs=[
            q_blk(0), k_blk(1), k_blk(2), q_blk(3),
            vec(DA_HEAD_DIM), vec(DA_HEAD_DIM), vec(DA_HEAD_DIM), vec(DA_HEAD_DIM), vec(DA_VALUE_DIM),
        ],
        out_specs=q_blk(0),
        scratch_shapes=[pltpu.VMEM((2, tq, LANES), F32), pltpu.VMEM((2, tq, LANES), F32),
                        pltpu.VMEM((2, tq, DA_VALUE_DIM), F32)],
    )
    return pl.pallas_call(
        kern,
        grid_spec=grid_spec,
        out_shape=jax.ShapeDtypeStruct((m, heads * DA_VALUE_DIM), BF16),
        compiler_params=_params(("parallel", "parallel", "arbitrary")),
        name="flash_diff_attention",
    )(qtab, ktab, kinds, proj, proj, proj, proj, lam_q1.reshape(1, -1), lam_k1.reshape(1, -1),
      lam_q2.reshape(1, -1), lam_k2.reshape(1, -1), subln_w.reshape(1, -1))


def _unit_lower_inverse(lows, row, col):
    blk = lambda sh: (row >> sh) == (col >> sh)
    eye = jnp.where(row == col, 1.0, 0.0)
    d1 = [jnp.where(blk(3), lo, 0.0) for lo in lows]
    d1b = [d.astype(BF16) for d in d1]
    d2 = [_mm(d, d) for d in d1b]
    d2b = [d.astype(BF16) for d in d2]
    d4 = [_mm(d, d) for d in d2b]
    xs = [eye + a + b + _mm(ab, bb) for a, b, ab, bb in zip(d1, d2, d1b, d2b)]
    xs = [x + _mm(x, d) for x, d in zip(xs, d4)]
    for sh in (3, 4, 5):
        off = jnp.logical_and(blk(sh + 1), jnp.logical_not(blk(sh)))
        ns = [jnp.where(off, lo, 0.0) for lo in lows]
        xb = [x.astype(BF16) for x in xs]
        ts = [_mm(x, n) for x, n in zip(xb, ns)]
        xs = [x + _mm(t, y) for x, t, y in zip(xs, ts, xb)]
    return xs


def _wkv_kernel(r_ref, k_ref, v_ref, g_ref, lora_ref, vfirst_ref, w0_ref, a0_ref, kk_ref, ka_ref, rk_ref, v0_ref,
                gnw_ref, gnb_ref, dup_ref, iup_ref, vup_ref, o_ref, s_ref, *, vres):
    c = pl.program_id(2)
    seqs = range(r_ref.shape[0])
    groups = r_ref.shape[2] // LANES
    ch = WKV_CHUNK
    n2 = 2 * ch

    @pl.when(c == 0)
    def _():
        s_ref[...] = jnp.zeros(s_ref.shape, F32)

    row = lax.broadcasted_iota(jnp.int32, (n2, n2), 0)
    col = lax.broadcasted_iota(jnp.int32, (n2, n2), 1)
    strict = col < row
    incl = col <= row
    lane = lax.broadcasted_iota(jnp.int32, (ch, LANES), 1)
    first = lane < RW_HEAD_DIM
    m0 = jnp.where(first, 1.0, 0.0)
    m1 = 1.0 - m0
    r64 = lax.broadcasted_iota(jnp.int32, (ch, ch), 0)
    c64 = lax.broadcasted_iota(jnp.int32, (ch, ch), 1)
    tri = jnp.where(c64 <= r64, 1.0, 0.0).astype(BF16)
    sls = [slice(u * LANES, (u + 1) * LANES) for u in range(groups)]

    def segsum(x):
        s0 = jnp.sum(x * m0, axis=-1, keepdims=True)
        s1 = jnp.sum(x * m1, axis=-1, keepdims=True)
        return jnp.where(first, s0, s1)

    def stack2(x):
        return jnp.concatenate([jnp.where(first, x, 0), jnp.where(first, 0, x)], axis=0)

    a2, b2, k2, r2, v2, g_end, bonus_in, gate = [], [], [], [], [], [], [], []
    for bi in seqs:
        lora = lora_ref[bi]
        dwt = jnp.tanh(lora[:, 0:LORA_PAD]).astype(BF16)
        da = lora[:, LORA_PAD:2 * LORA_PAD].astype(BF16)
        lw = -math.exp(-0.5) * _sigmoid(w0_ref[...] + _dot(dwt, dup_ref[...]))
        a = _sigmoid(a0_ref[...] + _dot(da, iup_ref[...]))
        r = r_ref[bi]
        k = k_ref[bi]
        v = v_ref[bi]
        if vres:
            pv = lora[:, 2 * LORA_PAD:3 * LORA_PAD].astype(BF16)
            v = v + (vfirst_ref[bi] - v) * _sigmoid(v0_ref[...] + _dot(pv, vup_ref[...]))
        kk = k * kk_ref[...]
        k_mod = k * (1.0 + (a - 1.0) * ka_ref[...])
        rkr = r * k_mod * rk_ref[...]
        lw_hi, lw_lo = _split2(lw)
        cum = _dot(tri, lw_hi) + _dot(tri, lw_lo)
        g_in = jnp.exp(cum)
        g_inv = jnp.exp(-cum)
        g_ex = jnp.exp(cum - lw)
        rt = (r * g_in).astype(BF16)
        kt = (k_mod * g_inv).astype(BF16)
        vb = v.astype(BF16)
        for sl in sls:
            kc = kk[:, sl]
            kn = kc * lax.rsqrt(jnp.maximum(segsum(kc * kc), 1e-24))
            a2.append(stack2((-kn * g_ex[:, sl]).astype(BF16)))
            b2.append(stack2((kn * a[:, sl] * g_inv[:, sl]).astype(BF16)))
            k2.append(stack2(kt[:, sl]))
            r2.append(stack2(rt[:, sl]))
            v2.append(stack2(vb[:, sl]))
            g_end.append(g_in[ch - 1:ch, sl])
            bonus_in.append((rkr[:, sl], v[:, sl]))
            gate.append(g_ref[bi, :, sl])
    units = range(len(a2))

    bk = [jnp.concatenate([b, k], axis=0) for b, k in zip(b2, k2)]
    sc = [_mm(jnp.concatenate([a, r], axis=0), x, nt=True) for a, r, x in zip(a2, r2, bk)]
    low_ab = [jnp.where(strict, s[:n2, :n2], 0.0) for s in sc]
    low_ak = [jnp.where(strict, s[:n2, n2:], 0.0) for s in sc]
    low_rb = [jnp.where(incl, s[n2:, :n2], 0.0) for s in sc]
    low_rk = [jnp.where(incl, s[n2:, n2:], 0.0) for s in sc]

    w1 = [_mm(l, v) for l, v in zip(low_ak, v2)]
    rkv = [_mm(l, v) for l, v in zip(low_rk, v2)]
    tinv = _unit_lower_inverse(low_ab, row, col)
    au = [_mm(t, jnp.concatenate([a, w.astype(BF16)], axis=1)) for t, a, w in zip(tinv, a2, w1)]

    ss = [s_ref[u] for u in units]
    ars = [_mm(jnp.concatenate([x[:, :n2].astype(BF16), r], axis=0), s, nt=True) for x, r, s in zip(au, r2, ss)]
    us = [x[:n2] + y[:, n2:] for x, y in zip(ars, au)]
    y2 = [x[n2:] + _mm(l, u) + z for x, l, u, z in zip(ars, low_rb, us, rkv)]
    upd = [_mm(jnp.concatenate([u, v.astype(F32)], axis=0).T, y) for u, v, y in zip(us, v2, bk)]
    for u in units:
        s_ref[u] = (ss[u] + upd[u]) * g_end[u]

    inv_n = 1.0 / RW_HEAD_DIM
    ys = [y[:ch] + y[ch:] for y in y2]
    ds = [y - segsum(y) * inv_n for y in ys]
    var = [segsum(d * d) * inv_n for d in ds]
    for u in units:
        bi, sl = u // groups, sls[u % groups]
        yn = ds[u] * lax.rsqrt(var[u] + GN_EPS) * gnw_ref[:, sl] + gnb_ref[:, sl]
        rkr, v = bonus_in[u]
        g = gate[u]
        o_ref[bi, :, sl] = ((yn + segsum(rkr) * v) * (g * _sigmoid(g))).astype(BF16)


def _wkv(pm, lora, pm_first, w0, a0, k_k, k_a, r_k, v0, gn_w, gn_b, dup, iup, vup, *, batch, seq, width, vres):
    m = pm.shape[0]
    nc = seq // WKV_CHUNK
    tc = min(WKV_GROUPS * LANES, width)
    ncb = width // tc
    nb = math.gcd(WKV_SEQS, batch)
    view = lambda a: a.reshape(batch, seq, a.shape[1])
    col = lambda c0: pl.BlockSpec((nb, WKV_CHUNK, tc), lambda b, j, c: (b, c, c0 * ncb + j))
    vec = pl.BlockSpec((1, tc), lambda b, j, c: (0, j))
    up = pl.BlockSpec((LORA_PAD, tc), lambda b, j, c: (0, j))
    row = lambda a: a.reshape(1, width)
    out = pl.pallas_call(
        functools.partial(_wkv_kernel, vres=vres),
        grid=(batch // nb, ncb, nc),
        in_specs=[col(0), col(1), col(2), col(3),
                  pl.BlockSpec((nb, WKV_CHUNK, lora.shape[1]), lambda b, j, c: (b, c, 0)), col(2),
                  vec, vec, vec, vec, vec, vec, vec, vec, up, up, up],
        out_specs=pl.BlockSpec((nb, WKV_CHUNK, tc), lambda b, j, c: (b, c, j)),
        out_shape=jax.ShapeDtypeStruct((batch, seq, width), BF16),
        scratch_shapes=[pltpu.VMEM((nb * tc // LANES, 2 * WKV_CHUNK, 2 * WKV_CHUNK), F32)],
        compiler_params=_params(("parallel", "parallel", "arbitrary")),
        name="wkv7_chunked",
    )(view(pm), view(pm), view(pm), view(pm), view(lora), view(pm_first), row(w0), row(a0), row(k_k), row(k_a),
      row(r_k), row(v0), row(gn_w), row(gn_b), dup, iup, vup)
    return out.reshape(m, width)


def _tiles(m, seq):
    return dict(tm=min(1024, seq), tn=min(1024, 4 * DA_VALUE_DIM), tr=min(512, seq), tq=min(512, seq))


def _attention_layer(x, cos, sin, norm_w, w_in, q_gain, k_gain, lq1, lk1, lq2, lk2, subln_w, w_out, lambda_init,
                     *, batch, seq, t):
    d = x.shape[1]
    heads = d // DA_VALUE_DIM
    proj = _attn_in_proj(x, norm_w, w_in.astype(BF16), cos, sin, q_gain, k_gain, tm=t["tm"], tn=min(t["tn"], d))
    o = _flash_diff_attention(proj, lq1, lk1, lq2, lk2, subln_w, batch=batch, seq=seq, heads=heads,
                              lambda_init=lambda_init, tq=t["tq"])
    return _matmul_residual(o, w_out.astype(BF16), x, tm=t["tm"], tn=min(t["tn"], d))


def _pad_rows(w, rows):
    return jnp.pad(w, ((0, rows - w.shape[0]), (0, 0)))


def _pad_cols(w, cols):
    return jnp.pad(w, ((0, 0), (0, cols - w.shape[1])))


def _rwkv_layer(x, pm_first, norm_w, w_in, mu, w0, decay_up, a0, iclr_up, k_k, k_a, r_k, gn_w, gn_b, w_out, vres,
                *, batch, seq, t):
    d = x.shape[1]
    wide = 4 * d
    dr = decay_up.shape[0]
    ir = iclr_up.shape[0]
    lora_w = [_pad_cols(w_in[:, wide:wide + dr], LORA_PAD), _pad_cols(w_in[:, wide + dr:wide + dr + ir], LORA_PAD)]
    lora_mu = [jnp.pad(mu[wide:wide + dr], (0, LORA_PAD - dr)), jnp.pad(mu[wide + dr:wide + dr + ir], (0, LORA_PAD - ir))]
    if vres is not None:
        vd_w, vd_mu, v0, vu_w = vres
        lora_w.append(_pad_cols(vd_w, LORA_PAD))
        lora_mu.append(jnp.pad(vd_mu, (0, LORA_PAD - vd_mu.shape[0])))
        vup = _pad_rows(vu_w, LORA_PAD).astype(BF16)
    else:
        v0 = jnp.zeros((d,), F32)
        vup = jnp.zeros((LORA_PAD, d), BF16)
    lora_w = jnp.concatenate(lora_w, axis=1).astype(BF16)
    lora_mu = jnp.concatenate(lora_mu)
    pm, lora = _rwkv_in_proj(x, norm_w, w_in.astype(BF16), mu[:wide], lora_w, lora_mu, rows_per_seq=seq,
                             tm=t["tm"], tn=t["tn"])
    o = _wkv(pm, lora, pm if pm_first is None else pm_first, w0, a0, k_k, k_a, r_k.reshape(-1), v0, gn_w, gn_b,
             _pad_rows(decay_up, LORA_PAD).astype(BF16), _pad_rows(iclr_up, LORA_PAD).astype(BF16), vup,
             batch=batch, seq=seq, width=d, vres=vres is not None)
    return _matmul_residual(o, w_out.astype(BF16), x, tm=t["tm"], tn=min(t["tn"], d)), pm


def kernel(x, positions, norm_w, da_w_in, da_q_gain, da_k_gain, da_lam_q1, da_lam_k1, da_lam_q2, da_lam_k2, da_subln_w, da_w_out, rw_w_in, rw_mu, rw_w0, rw_decay_up, rw_a0, rw_iclr_up, rw_k_k, rw_k_a, rw_r_k, rw_gn_w, rw_gn_b, rw_w_out, rw_vres_down, rw_vres_mu, rw_v0, rw_vres_up):
    batch, seq, d = x.shape
    depth = norm_w.shape[0]
    m = batch * seq
    t = _tiles(m, seq)
    xf = x.reshape(m, d)
    cos, sin = _rope_tables(positions.reshape(m), tr=t["tr"])
    pm_first = None
    for i in range(depth):
        j = i // 2
        if i % 2 == 0:
            lambda_init = 0.8 - 0.6 * math.exp(-0.3 * i)
            xf = _attention_layer(xf, cos, sin, norm_w[i], da_w_in[j], da_q_gain[j], da_k_gain[j], da_lam_q1[j],
                                  da_lam_k1[j], da_lam_q2[j], da_lam_k2[j], da_subln_w[j], da_w_out[j], lambda_init,
                                  batch=batch, seq=seq, t=t)
        else:
            vres = None if j == 0 else (rw_vres_down[j - 1], rw_vres_mu[j - 1], rw_v0[j - 1], rw_vres_up[j - 1])
            xf, pm_now = _rwkv_layer(xf, pm_first, norm_w[i], rw_w_in[j], rw_mu[j], rw_w0[j], rw_decay_up[j], rw_a0[j],
                                     rw_iclr_up[j], rw_k_k[j], rw_k_a[j], rw_r_k[j], rw_gn_w[j], rw_gn_b[j],
                                     rw_w_out[j], vres, batch=batch, seq=seq, t=t)
            if pm_first is None:
                pm_first = pm_now
    return xf.reshape(batch, seq, d)
```

```python
import functools
import math

import jax
import jax.numpy as jnp
from jax import lax
from jax.experimental import pallas as pl
from jax.experimental.pallas import tpu as pltpu

F32 = jnp.float32
BF16 = jnp.bfloat16

LANES = 128
DA_HEAD_DIM = 128
DA_VALUE_DIM = 2 * DA_HEAD_DIM
RW_HEAD_DIM = 64
WKV_CHUNK = 64
FLASH_GRANULES = 2
WKV_GROUPS = 16
WKV_SEQS = 2
LORA_PAD = 128
ROPE_THETA = 10000.0
NORM_EPS = 1e-6
GN_EPS = 64e-5
NEG_BIG = -1e30
VMEM_LIMIT = 56 * 1024 * 1024


def _params(sem):
    return pltpu.CompilerParams(dimension_semantics=sem, vmem_limit_bytes=VMEM_LIMIT)


def _dot(a, b):
    return jnp.dot(a, b, preferred_element_type=F32)


def _dot_nt(a, b):
    return lax.dot_general(a, b, (((1,), (1,)), ((), ())), preferred_element_type=F32)


def _split2(x):
    hi = x.astype(BF16)
    lo = (x - hi.astype(F32)).astype(BF16)
    return hi, lo


def _mm(a, b, nt=False):
    d = _dot_nt if nt else _dot
    return d(a.astype(BF16), b.astype(BF16))


def _sigmoid(x):
    return 1.0 / (1.0 + jnp.exp(-x))


def _store_normed(x_ref, g_ref, xn_ref):
    x = x_ref[...]
    ms = jnp.mean(x * x, axis=-1, keepdims=True)
    xn_ref[...] = (x * lax.rsqrt(ms + NORM_EPS) * g_ref[...]).astype(BF16)


def _rwkv_in_kernel(x_ref, g_ref, w_ref, mu_ref, lw_ref, lmu_ref, o_ref, lo_ref, xn_ref, prev_ref, lprev_ref,
                    *, rows_per_seq):
    i = pl.program_id(0)
    j = pl.program_id(1)
    tm = x_ref.shape[0]
    seq_start = (i * tm) % rows_per_seq == 0

    def shift_mix(acc, prev, mu):
        row = lax.broadcasted_iota(jnp.int32, acc.shape, 0)
        shifted = jnp.where(row == 0, jnp.where(seq_start, 0.0, prev), pltpu.roll(acc, 1, axis=0))
        return acc + (shifted - acc) * mu

    @pl.when(i == 0)
    def _():
        prev_ref[j] = jnp.zeros(prev_ref.shape[1:], F32)

    @pl.when(jnp.logical_and(i == 0, j == 0))
    def _():
        lprev_ref[...] = jnp.zeros(lprev_ref.shape, F32)

    @pl.when(j == 0)
    def _():
        _store_normed(x_ref, g_ref, xn_ref)
        lacc = _dot(xn_ref[...], lw_ref[...])
        lo_ref[...] = shift_mix(lacc, lprev_ref[...], lmu_ref[...])
        lprev_ref[...] = lacc[tm - 1:tm, :]

    acc = _dot(xn_ref[...], w_ref[...])
    o_ref[...] = shift_mix(acc, prev_ref[j], mu_ref[...])
    prev_ref[j] = acc[tm - 1:tm, :]


def _rwkv_in_proj(x, gain, w_bf16, mu, lora_w, lora_mu, *, rows_per_seq, tm, tn):
    m, d = x.shape
    n = mu.shape[0]
    nl = lora_w.shape[1]
    assert m % tm == 0 and n % tn == 0 and rows_per_seq % tm == 0 and n <= w_bf16.shape[1]
    return pl.pallas_call(
        functools.partial(_rwkv_in_kernel, rows_per_seq=rows_per_seq),
        grid=(m // tm, n // tn),
        in_specs=[
            pl.BlockSpec((tm, d), lambda i, j: (i, 0)),
            pl.BlockSpec((1, d), lambda i, j: (0, 0)),
            pl.BlockSpec((d, tn), lambda i, j: (0, j)),
            pl.BlockSpec((1, tn), lambda i, j: (0, j)),
            pl.BlockSpec((d, nl), lambda i, j: (0, 0)),
            pl.BlockSpec((1, nl), lambda i, j: (0, 0)),
        ],
        out_specs=[pl.BlockSpec((tm, tn), lambda i, j: (i, j)), pl.BlockSpec((tm, nl), lambda i, j: (i, 0))],
        out_shape=[jax.ShapeDtypeStruct((m, n), F32), jax.ShapeDtypeStruct((m, nl), F32)],
        scratch_shapes=[pltpu.VMEM((tm, d), BF16), pltpu.VMEM((n // tn, 1, tn), F32), pltpu.VMEM((1, nl), F32)],
        compiler_params=_params(("arbitrary", "arbitrary")),
        name="rwkv_in_proj",
    )(x, gain.reshape(1, d), w_bf16, mu.reshape(1, n), lora_w, lora_mu.reshape(1, nl))


def _attn_in_kernel(x_ref, g_ref, w_ref, cos_ref, sin_ref, qg_ref, kg_ref, o_ref, xn_ref, acc_ref, *, sec, scale):
    j = pl.program_id(1)
    last = pl.num_programs(1) - 1

    def matmul():
        acc_ref[...] = _dot(xn_ref[...], w_ref[...])

    def finish_qk():
        is_q = j - 1 < sec
        gain = jnp.where(is_q, qg_ref[...], kg_ref[...])
        mult = jnp.where(is_q, scale, 1.0)
        cos = cos_ref[...]
        lane = lax.broadcasted_iota(jnp.int32, cos.shape, 1)
        sin_signed = jnp.where(lane < DA_HEAD_DIM // 2, -sin_ref[...], sin_ref[...])
        for c in range(acc_ref.shape[1] // DA_HEAD_DIM):
            sl = slice(c * DA_HEAD_DIM, (c + 1) * DA_HEAD_DIM)
            x = acc_ref[:, sl]
            y = x * lax.rsqrt(jnp.mean(x * x, axis=-1, keepdims=True) + NORM_EPS) * gain
            rot = pltpu.roll(y, DA_HEAD_DIM // 2, axis=1)
            o_ref[:, sl] = ((y * cos + rot * sin_signed) * mult).astype(BF16)

    def finish_plain():
        o_ref[...] = acc_ref[...].astype(BF16)

    @pl.when(j == 0)
    def _():
        _store_normed(x_ref, g_ref, xn_ref)
        matmul()

    @pl.when(jnp.logical_and(j >= 1, j <= 2 * sec))
    def _():
        finish_qk()
        matmul()

    @pl.when(jnp.logical_and(j > 2 * sec, j < last))
    def _():
        finish_plain()
        matmul()

    @pl.when(j == last)
    def _():
        finish_plain()


def _attn_in_proj(x, gain, w_bf16, cos, sin, q_gain, k_gain, *, tm, tn):
    m, d = x.shape
    n = w_bf16.shape[1]
    nj = n // tn
    assert m % tm == 0 and d % tn == 0 and n == 4 * d
    rope = pl.BlockSpec((tm, DA_HEAD_DIM), lambda i, j: (i, 0))
    vec = pl.BlockSpec((1, DA_HEAD_DIM), lambda i, j: (0, 0))
    return pl.pallas_call(
        functools.partial(_attn_in_kernel, sec=d // tn, scale=DA_HEAD_DIM ** -0.5 * math.log2(math.e)),
        grid=(m // tm, nj + 1),
        in_specs=[
            pl.BlockSpec((tm, d), lambda i, j: (i, 0)),
            pl.BlockSpec((1, d), lambda i, j: (0, 0)),
            pl.BlockSpec((d, tn), lambda i, j: (0, jnp.minimum(j, nj - 1))),
            rope, rope, vec, vec,
        ],
        out_specs=pl.BlockSpec((tm, tn), lambda i, j: (i, jnp.maximum(j - 1, 0))),
        out_shape=jax.ShapeDtypeStruct((m, n), BF16),
        scratch_shapes=[pltpu.VMEM((tm, d), BF16), pltpu.VMEM((tm, tn), F32)],
        compiler_params=_params(("parallel", "arbitrary")),
        name="attn_in_proj",
    )(x, gain.reshape(1, d), w_bf16, cos, sin, q_gain.reshape(1, -1), k_gain.reshape(1, -1))


def _mm_res_kernel(a_ref, w_ref, x_ref, o_ref):
    o_ref[...] = x_ref[...] + _dot(a_ref[...], w_ref[...])


def _matmul_residual(a_bf16, w_bf16, x, *, tm, tn):
    m, k = a_bf16.shape
    n = w_bf16.shape[1]
    assert m % tm == 0 and n % tn == 0
    return pl.pallas_call(
        _mm_res_kernel,
        grid=(m // tm, n // tn),
        in_specs=[
            pl.BlockSpec((tm, k), lambda i, j: (i, 0)),
            pl.BlockSpec((k, tn), lambda i, j: (0, j)),
            pl.BlockSpec((tm, tn), lambda i, j: (i, j)),
        ],
        out_specs=pl.BlockSpec((tm, tn), lambda i, j: (i, j)),
        out_shape=jax.ShapeDtypeStruct((m, n), F32),
        compiler_params=_params(("parallel", "arbitrary")),
        name="out_proj_residual",
    )(a_bf16, w_bf16, x)


def _rope_kernel(pos_ref, inv_ref, cos_ref, sin_ref):
    ang = pos_ref[...].astype(F32) * inv_ref[...]
    cos_ref[...] = jnp.cos(ang)
    sin_ref[...] = jnp.sin(ang)


def _rope_tables(positions_flat, *, tr):
    m = positions_flat.shape[0]
    inv = ROPE_THETA ** (-jnp.arange(0, DA_HEAD_DIM, 2, dtype=F32) / DA_HEAD_DIM)
    inv = jnp.concatenate([inv, inv]).reshape(1, DA_HEAD_DIM)
    return pl.pallas_call(
        _rope_kernel,
        grid=(m // tr,),
        in_specs=[pl.BlockSpec((tr, 1), lambda i: (i, 0)), pl.BlockSpec((1, DA_HEAD_DIM), lambda i: (0, 0))],
        out_specs=[pl.BlockSpec((tr, DA_HEAD_DIM), lambda i: (i, 0))] * 2,
        out_shape=[jax.ShapeDtypeStruct((m, DA_HEAD_DIM), F32)] * 2,
        compiler_params=_params(("parallel",)),
        name="rope_tables",
    )(positions_flat.reshape(m, 1), inv)


def _flash_kernel(qtab_ref, ktab_ref, kind_ref, q_ref, k_ref, v_ref, gate_ref, lq1_ref, lk1_ref, lq2_ref, lk2_ref,
                  sub_ref, o_ref, m_ref, l_ref, acc_ref, *, lambda_init, modes):
    step_id = pl.program_id(2)
    mode = kind_ref[step_id]
    tq = q_ref.shape[0]
    per_block = k_ref.shape[0] // tq
    comps = range(2)
    sls = [slice(c * DA_HEAD_DIM, (c + 1) * DA_HEAD_DIM) for c in comps]

    def process(granules, diag, first):
        below = (granules - 1) * tq if diag else granules * tq
        spans = ([(0, below, False)] if below else []) + ([(below, below + tq, True)] if diag else [])
        s = [[_dot_nt(q_ref[:, sl], k_ref[lo:hi, sl]) for lo, hi, _ in spans] for sl in sls]
        if diag:
            row = lax.broadcasted_iota(jnp.int32, (tq, tq), 0)
            col = lax.broadcasted_iota(jnp.int32, (tq, tq), 1)
            keep = col <= row
            s = [xs[:-1] + [jnp.where(keep, xs[-1], NEG_BIG)] for xs in s]
        m = []
        for c in comps:
            m_new = jnp.full((tq, LANES), NEG_BIG, F32) if first else m_ref[c]
            for x in s[c]:
                m_new = jnp.maximum(m_new, jnp.max(x, axis=-1, keepdims=True))
            m.append(m_new)
        if not first:
            alpha = [jnp.exp2(m_ref[c] - m[c]) for c in comps]
        p = [[jnp.exp2(x - jnp.concatenate([m[c]] * ((hi - lo) // LANES), axis=1))
              for x, (lo, hi, _) in zip(s[c], spans)] for c in comps]
        pv = [[_dot(x.astype(BF16), v_ref[lo:hi, :]) for x, (lo, hi, _) in zip(p[c], spans)] for c in comps]
        l, acc = [], []
        for c in comps:
            parts = [x[:, j * LANES:(j + 1) * LANES] for x in p[c] for j in range(x.shape[1] // LANES)]
            if first:
                l.append(sum(parts[1:], parts[0]))
                acc.append(sum(pv[c][1:], pv[c][0]))
            else:
                l.append(sum(parts, alpha[c] * l_ref[c]))
                acc.append(sum(pv[c], jnp.concatenate([alpha[c]] * (DA_VALUE_DIM // LANES), axis=1) * acc_ref[c]))
        return m, l, acc

    def carry(m, l, acc):
        for c in comps:
            m_ref[c] = m[c]
            l_ref[c] = l[c]
            acc_ref[c] = acc[c]

    def finish(l, acc):
        lam = (jnp.exp(jnp.sum(lq1_ref[...] * lk1_ref[...], axis=-1, keepdims=True))
               - jnp.exp(jnp.sum(lq2_ref[...] * lk2_ref[...], axis=-1, keepdims=True)) + lambda_init)
        l0 = jnp.sum(l[0], axis=-1, keepdims=True)
        l1 = jnp.sum(l[1], axis=-1, keepdims=True)
        o = acc[0] / l0 - lam * (acc[1] / l1)
        o = o * lax.rsqrt(jnp.mean(o * o, axis=-1, keepdims=True) + NORM_EPS) * sub_ref[...]
        g = gate_ref[...].astype(F32)
        o_ref[...] = (o * (1.0 - lambda_init) * (g * _sigmoid(g))).astype(BF16)

    for code, (kind, first) in enumerate(modes):
        @pl.when(mode == code)
        def _(kind=kind, first=first):
            if kind == 0:
                carry(*process(per_block, False, first))
            else:
                _, l, acc = process(kind, True, first)
                finish(l, acc)


def _flash_diff_attention(proj, lam_q1, lam_k1, lam_q2, lam_k2, subln_w, *, batch, seq, heads,
                          lambda_init, tq):
    m = proj.shape[0]
    nq = seq // tq
    per = min(FLASH_GRANULES, nq)
    nkb = nq // per
    assert nq % per == 0
    steps = [(a, b, 0 if b < a // per else 1 + a % per) for a in range(nq) for b in range(a // per + 1)]
    modes = sorted({(kind, b == 0) for _, b, kind in steps})
    qtab = jnp.asarray([a for a, _, _ in steps], jnp.int32)
    ktab = jnp.asarray([b for _, b, _ in steps], jnp.int32)
    kinds = jnp.asarray([modes.index((kind, b == 0)) for _, b, kind in steps], jnp.int32)
    q_blk = lambda c0: pl.BlockSpec((tq, DA_VALUE_DIM),
                                    lambda b, h, s, qt, kt, kd: (b * nq + qt[s], c0 * heads + h))
    k_blk = lambda c0: pl.BlockSpec((per * tq, DA_VALUE_DIM),
                                    lambda b, h, s, qt, kt, kd: (b * nkb + kt[s], c0 * heads + h))
    vec = lambda n: pl.BlockSpec((1, n), lambda b, h, s, qt, kt, kd: (0, 0))
    kern = functools.partial(_flash_kernel, lambda_init=lambda_init, modes=modes)
    grid_spec = pltpu.PrefetchScalarGridSpec(
        num_scalar_prefetch=3,
        grid=(batch, heads, len(steps)),
        in_specs=[
            q_blk(0), k_blk(1), k_blk(2), q_blk(3),
            vec(DA_HEAD_DIM), vec(DA_HEAD_DIM), vec(DA_HEAD_DIM), vec(DA_HEAD_DIM), vec(DA_VALUE_DIM),
        ],
        out_specs=q_blk(0),
        scratch_shapes=[pltpu.VMEM((2, tq, LANES), F32), pltpu.VMEM((2, tq, LANES), F32),
                        pltpu.VMEM((2, tq, DA_VALUE_DIM), F32)],
    )
    return pl.pallas_call(
        kern,
        grid_spec=grid_spec,
        out_shape=jax.ShapeDtypeStruct((m, heads * DA_VALUE_DIM), BF16),
        compiler_params=_params(("parallel", "parallel", "arbitrary")),
        name="flash_diff_attention",
    )(qtab, ktab, kinds, proj, proj, proj, proj, lam_q1.reshape(1, -1), lam_k1.reshape(1, -1),
      lam_q2.reshape(1, -1), lam_k2.reshape(1, -1), subln_w.reshape(1, -1))


def _unit_lower_inverse(lows, row, col):
    blk = lambda sh: (row >> sh) == (col >> sh)
    eye = jnp.where(row == col, 1.0, 0.0)
    d1 = [jnp.where(blk(3), lo, 0.0) for lo in lows]
    d1b = [d.astype(BF16) for d in d1]
    d2 = [_mm(d, d) for d in d1b]
    d2b = [d.astype(BF16) for d in d2]
    d4 = [_mm(d, d) for d in d2b]
    xs = [eye + a + b + _mm(ab, bb) for a, b, ab, bb in zip(d1, d2, d1b, d2b)]
    xs = [x + _mm(x, d) for x, d in zip(xs, d4)]
    for sh in (3, 4, 5):
        off = jnp.logical_and(blk(sh + 1), jnp.logical_not(blk(sh)))
        ns = [jnp.where(off, lo, 0.0) for lo in lows]
        xb = [x.astype(BF16) for x in xs]
        ts = [_mm(x, n) for x, n in zip(xb, ns)]
        xs = [x + _mm(t, y) for x, t, y in zip(xs, ts, xb)]
    return xs


def _wkv_kernel(r_ref, k_ref, v_ref, g_ref, lora_ref, vfirst_ref, w0_ref, a0_ref, kk_ref, ka_ref, rk_ref, v0_ref,
                gnw_ref, gnb_ref, dup_ref, iup_ref, vup_ref, o_ref, s_ref, *, vres):
    c = pl.program_id(2)
    seqs = range(r_ref.shape[0])
    groups = r_ref.shape[2] // LANES
    ch = WKV_CHUNK
    n2 = 2 * ch

    @pl.when(c == 0)
    def _():
        s_ref[...] = jnp.zeros(s_ref.shape, F32)

    row = lax.broadcasted_iota(jnp.int32, (n2, n2), 0)
    col = lax.broadcasted_iota(jnp.int32, (n2, n2), 1)
    strict = col < row
    incl = col <= row
    lane = lax.broadcasted_iota(jnp.int32, (ch, LANES), 1)
    first = lane < RW_HEAD_DIM
    m0 = jnp.where(first, 1.0, 0.0)
    m1 = 1.0 - m0
    r64 = lax.broadcasted_iota(jnp.int32, (ch, ch), 0)
    c64 = lax.broadcasted_iota(jnp.int32, (ch, ch), 1)
    tri = jnp.where(c64 <= r64, 1.0, 0.0).astype(BF16)
    sls = [slice(u * LANES, (u + 1) * LANES) for u in range(groups)]

    def segsum(x):
        s0 = jnp.sum(x * m0, axis=-1, keepdims=True)
        s1 = jnp.sum(x * m1, axis=-1, keepdims=True)
        return jnp.where(first, s0, s1)

    def stack2(x):
        return jnp.concatenate([jnp.where(first, x, 0), jnp.where(first, 0, x)], axis=0)

    a2, b2, k2, r2, v2, g_end, bonus_in, gate = [], [], [], [], [], [], [], []
    for bi in seqs:
        lora = lora_ref[bi]
        dwt = jnp.tanh(lora[:, 0:LORA_PAD]).astype(BF16)
        da = lora[:, LORA_PAD:2 * LORA_PAD].astype(BF16)
        lw = -math.exp(-0.5) * _sigmoid(w0_ref[...] + _dot(dwt, dup_ref[...]))
        a = _sigmoid(a0_ref[...] + _dot(da, iup_ref[...]))
        r = r_ref[bi]
        k = k_ref[bi]
        v = v_ref[bi]
        if vres:
            pv = lora[:, 2 * LORA_PAD:3 * LORA_PAD].astype(BF16)
            v = v + (vfirst_ref[bi] - v) * _sigmoid(v0_ref[...] + _dot(pv, vup_ref[...]))
        kk = k * kk_ref[...]
        k_mod = k * (1.0 + (a - 1.0) * ka_ref[...])
        rkr = r * k_mod * rk_ref[...]
        lw_hi, lw_lo = _split2(lw)
        cum = _dot(tri, lw_hi) + _dot(tri, lw_lo)
        g_in = jnp.exp(cum)
        g_inv = jnp.exp(-cum)
        g_ex = jnp.exp(cum - lw)
        rt = (r * g_in).astype(BF16)
        kt = (k_mod * g_inv).astype(BF16)
        vb = v.astype(BF16)
        for sl in sls:
            kc = kk[:, sl]
            kn = kc * lax.rsqrt(jnp.maximum(segsum(kc * kc), 1e-24))
            a2.append(stack2((-kn * g_ex[:, sl]).astype(BF16)))
            b2.append(stack2((kn * a[:, sl] * g_inv[:, sl]).astype(BF16)))
            k2.append(stack2(kt[:, sl]))
            r2.append(stack2(rt[:, sl]))
            v2.append(stack2(vb[:, sl]))
            g_end.append(g_in[ch - 1:ch, sl])
            bonus_in.append((rkr[:, sl], v[:, sl]))
            gate.append(g_ref[bi, :, sl])
    units = range(len(a2))

    bk = [jnp.concatenate([b, k], axis=0) for b, k in zip(b2, k2)]
    sc = [_mm(jnp.concatenate([a, r], axis=0), x, nt=True) for a, r, x in zip(a2, r2, bk)]
    low_ab = [jnp.where(strict, s[:n2, :n2], 0.0) for s in sc]
    low_ak = [jnp.where(strict, s[:n2, n2:], 0.0) for s in sc]
    low_rb = [jnp.where(incl, s[n2:, :n2], 0.0) for s in sc]
    low_rk = [jnp.where(incl, s[n2:, n2:], 0.0) for s in sc]

    w1 = [_mm(l, v) for l, v in zip(low_ak, v2)]
    rkv = [_mm(l, v) for l, v in zip(low_rk, v2)]
    tinv = _unit_lower_inverse(low_ab, row, col)
    au = [_mm(t, jnp.concatenate([a, w.astype(BF16)], axis=1)) for t, a, w in zip(tinv, a2, w1)]

    ss = [s_ref[u] for u in units]
    ars = [_mm(jnp.concatenate([x[:, :n2].astype(BF16), r], axis=0), s, nt=True) for x, r, s in zip(au, r2, ss)]
    us = [x[:n2] + y[:, n2:] for x, y in zip(ars, au)]
    y2 = [x[n2:] + _mm(l, u) + z for x, l, u, z in zip(ars, low_rb, us, rkv)]
    upd = [_mm(jnp.concatenate([u, v.astype(F32)], axis=0).T, y) for u, v, y in zip(us, v2, bk)]
    for u in units:
        s_ref[u] = (ss[u] + upd[u]) * g_end[u]

    inv_n = 1.0 / RW_HEAD_DIM
    ys = [y[:ch] + y[ch:] for y in y2]
    ds = [y - segsum(y) * inv_n for y in ys]
    var = [segsum(d * d) * inv_n for d in ds]
    for u in units:
        bi, sl = u // groups, sls[u % groups]
        yn = ds[u] * lax.rsqrt(var[u] + GN_EPS) * gnw_ref[:, sl] + gnb_ref[:, sl]
        rkr, v = bonus_in[u]
        g = gate[u]
        o_ref[bi, :, sl] = ((yn + segsum(rkr) * v) * (g * _sigmoid(g))).astype(BF16)


def _wkv(pm, lora, pm_first, w0, a0, k_k, k_a, r_k, v0, gn_w, gn_b, dup, iup, vup, *, batch, seq, width, vres):
    m = pm.shape[0]
    nc = seq // WKV_CHUNK
    tc = min(WKV_GROUPS * LANES, width)
    ncb = width // tc
    nb = math.gcd(WKV_SEQS, batch)
    view = lambda a: a.reshape(batch, seq, a.shape[1])
    col = lambda c0: pl.BlockSpec((nb, WKV_CHUNK, tc), lambda b, j, c: (b, c, c0 * ncb + j))
    vec = pl.BlockSpec((1, tc), lambda b, j, c: (0, j))
    up = pl.BlockSpec((LORA_PAD, tc), lambda b, j, c: (0, j))
    row = lambda a: a.reshape(1, width)
    out = pl.pallas_call(
        functools.partial(_wkv_kernel, vres=vres),
        grid=(batch // nb, ncb, nc),
        in_specs=[col(0), col(1), col(2), col(3),
                  pl.BlockSpec((nb, WKV_CHUNK, lora.shape[1]), lambda b, j, c: (b, c, 0)), col(2),
                  vec, vec, vec, vec, vec, vec, vec, vec, up, up, up],
        out_specs=pl.BlockSpec((nb, WKV_CHUNK, tc), lambda b, j, c: (b, c, j)),
        out_shape=jax.ShapeDtypeStruct((batch, seq, width), BF16),
        scratch_shapes=[pltpu.VMEM((nb * tc // LANES, 2 * WKV_CHUNK, 2 * WKV_CHUNK), F32)],
        compiler_params=_params(("parallel", "parallel", "arbitrary")),
        name="wkv7_chunked",
    )(view(pm), view(pm), view(pm), view(pm), view(lora), view(pm_first), row(w0), row(a0), row(k_k), row(k_a),
      row(r_k), row(v0), row(gn_w), row(gn_b), dup, iup, vup)
    return out.reshape(m, width)


def _tiles(m, seq):
    return dict(tm=min(1024, seq), tn=min(1024, 4 * DA_VALUE_DIM), tr=min(512, seq), tq=min(1024, seq))


def _attention_layer(x, cos, sin, norm_w, w_in, q_gain, k_gain, lq1, lk1, lq2, lk2, subln_w, w_out, lambda_init,
                     *, batch, seq, t):
    d = x.shape[1]
    heads = d // DA_VALUE_DIM
    proj = _attn_in_proj(x, norm_w, w_in.astype(BF16), cos, sin, q_gain, k_gain, tm=t["tm"], tn=min(t["tn"], d))
    o = _flash_diff_attention(proj, lq1, lk1, lq2, lk2, subln_w, batch=batch, seq=seq, heads=heads,
                              lambda_init=lambda_init, tq=t["tq"])
    return _matmul_residual(o, w_out.astype(BF16), x, tm=t["tm"], tn=min(t["tn"], d))


def _pad_rows(w, rows):
    return jnp.pad(w, ((0, rows - w.shape[0]), (0, 0)))


def _pad_cols(w, cols):
    return jnp.pad(w, ((0, 0), (0, cols - w.shape[1])))


def _rwkv_layer(x, pm_first, norm_w, w_in, mu, w0, decay_up, a0, iclr_up, k_k, k_a, r_k, gn_w, gn_b, w_out, vres,
                *, batch, seq, t):
    d = x.shape[1]
    wide = 4 * d
    dr = decay_up.shape[0]
    ir = iclr_up.shape[0]
    lora_w = [_pad_cols(w_in[:, wide:wide + dr], LORA_PAD), _pad_cols(w_in[:, wide + dr:wide + dr + ir], LORA_PAD)]
    lora_mu = [jnp.pad(mu[wide:wide + dr], (0, LORA_PAD - dr)), jnp.pad(mu[wide + dr:wide + dr + ir], (0, LORA_PAD - ir))]
    if vres is not None:
        vd_w, vd_mu, v0, vu_w = vres
        lora_w.append(_pad_cols(vd_w, LORA_PAD))
        lora_mu.append(jnp.pad(vd_mu, (0, LORA_PAD - vd_mu.shape[0])))
        vup = _pad_rows(vu_w, LORA_PAD).astype(BF16)
    else:
        v0 = jnp.zeros((d,), F32)
        vup = jnp.zeros((LORA_PAD, d), BF16)
    lora_w = jnp.concatenate(lora_w, axis=1).astype(BF16)
    lora_mu = jnp.concatenate(lora_mu)
    pm, lora = _rwkv_in_proj(x, norm_w, w_in.astype(BF16), mu[:wide], lora_w, lora_mu, rows_per_seq=seq,
                             tm=t["tm"], tn=t["tn"])
    o = _wkv(pm, lora, pm if pm_first is None else pm_first, w0, a0, k_k, k_a, r_k.reshape(-1), v0, gn_w, gn_b,
             _pad_rows(decay_up, LORA_PAD).astype(BF16), _pad_rows(iclr_up, LORA_PAD).astype(BF16), vup,
             batch=batch, seq=seq, width=d, vres=vres is not None)
    return _matmul_residual(o, w_out.astype(BF16), x, tm=t["tm"], tn=min(t["tn"], d)), pm


def kernel(x, positions, norm_w, da_w_in, da_q_gain, da_k_gain, da_lam_q1, da_lam_k1, da_lam_q2, da_lam_k2, da_subln_w, da_w_out, rw_w_in, rw_mu, rw_w0, rw_decay_up, rw_a0, rw_iclr_up, rw_k_k, rw_k_a, rw_r_k, rw_gn_w, rw_gn_b, rw_w_out, rw_vres_down, rw_vres_mu, rw_v0, rw_vres_up):
    batch, seq, d = x.shape
    depth = norm_w.shape[0]
    m = batch * seq
    t = _tiles(m, seq)
    xf = x.reshape(m, d)
    cos, sin = _rope_tables(positions.reshape(m), tr=t["tr"])
    pm_first = None
    for i in range(depth):
        j = i // 2
        if i % 2 == 0:
            lambda_init = 0.8 - 0.6 * math.exp(-0.3 * i)
            xf = _attention_layer(xf, cos, sin, norm_w[i], da_w_in[j], da_q_gain[j], da_k_gain[j], da_lam_q1[j],
                                  da_lam_k1[j], da_lam_q2[j], da_lam_k2[j], da_subln_w[j], da_w_out[j], lambda_init,
                                  batch=batch, seq=seq, t=t)
        else:
            vres = None if j == 0 else (rw_vres_down[j - 1], rw_vres_mu[j - 1], rw_v0[j - 1], rw_vres_up[j - 1])
            xf, pm_now = _rwkv_layer(xf, pm_first, norm_w[i], rw_w_in[j], rw_mu[j], rw_w0[j], rw_decay_up[j], rw_a0[j],
                                     rw_iclr_up[j], rw_k_k[j], rw_k_a[j], rw_r_k[j], rw_gn_w[j], rw_gn_b[j],
                                     rw_w_out[j], vres, batch=batch, seq=seq, t=t)
            if pm_first is None:
                pm_first = pm_now
    return xf.reshape(batch, seq, d)
```
